```python
import math
import jax, jax.numpy as jnp
from jax import lax
import numpy as np

D_MODEL = 1024
BATCH = 16
SEQ = 4096
DEPTH = 2

N_MIXERS = 2
PLE_DIM = 256
NORM_EPS = 1e-6
LN_EPS = 1e-5
GDN_HEADS = 8
GDN_DK = 128
GDN_DV = 128
GDN_CONV = 4
GDN_CHUNK = 64
GDN_QK = GDN_HEADS * GDN_DK
GDN_V = GDN_HEADS * GDN_DV
GDN_CONV_DIM = 2 * GDN_QK + GDN_V
GDN_IN = GDN_CONV_DIM + GDN_V + 2 * GDN_HEADS
SGU_CHUNK = 128
SGU_GROUPS = 8
SGU_WIDTH = 2 * D_MODEL
SGU_GROUP_DIM = SGU_WIDTH // SGU_GROUPS
D_FF = 2816
FFN_CONV = 3

kernel_name = 'hybrid_gdn_sgu_block'


def rms_norm(x, g):
    xf = x.astype(jnp.float32)
    y = xf * lax.rsqrt(jnp.mean(xf * xf, axis=-1, keepdims=True) + NORM_EPS)
    return (y * g.astype(jnp.float32)).astype(x.dtype)


def layer_norm(x, g, b):
    xf = x.astype(jnp.float32)
    mu = jnp.mean(xf, axis=-1, keepdims=True)
    xc = xf - mu
    y = xc * lax.rsqrt(jnp.mean(xc * xc, axis=-1, keepdims=True) + LN_EPS)
    return (y * g.astype(jnp.float32) + b.astype(jnp.float32)).astype(x.dtype)


def causal_dwconv(x, w):
    width, ch = w.shape
    return lax.conv_general_dilated(
        x, w[:, None, :].astype(x.dtype), window_strides=(1,), padding=((width - 1, 0),),
        dimension_numbers=('NWC', 'WIO', 'NWC'), feature_group_count=ch)


def l2norm(x):
    return x * lax.rsqrt(jnp.sum(x * x, axis=-1, keepdims=True) + NORM_EPS)


def chunk_gated_delta_rule(q, k, v, beta, g):
    bsz, seq, nh, dk = q.shape
    dv = v.shape[-1]
    c = GDN_CHUNK
    n = seq // c

    def to_chunks(t):
        return t.reshape(bsz, n, c, nh, -1).transpose(0, 1, 3, 2, 4)

    q = to_chunks(q) * (dk ** -0.5)
    k = to_chunks(k)
    v = to_chunks(v)
    beta = beta.reshape(bsz, n, c, nh).transpose(0, 1, 3, 2)
    G = jnp.cumsum(g.reshape(bsz, n, c, nh).transpose(0, 1, 3, 2), axis=-1)

    idx = jnp.arange(c)
    lower = idx[:, None] >= idx[None, :]
    strict = idx[:, None] > idx[None, :]
    eye = jnp.eye(c, dtype=jnp.float32)
    decay = jnp.exp(jnp.where(lower, G[..., :, None] - G[..., None, :], -jnp.inf))

    kb = k * beta[..., None]
    kkt = jnp.einsum('bnhid,bnhjd->bnhij', kb, k) * decay
    a_mat = eye + jnp.where(strict, kkt, 0.0)
    t_mat = lax.linalg.triangular_solve(a_mat, jnp.broadcast_to(eye, a_mat.shape),
                                        left_side=True, lower=True, unit_diagonal=True)
    u = jnp.einsum('bnhij,bnhje->bnhie', t_mat, v * beta[..., None])
    w = jnp.einsum('bnhij,bnhjd->bnhid', t_mat, kb * jnp.exp(G)[..., None])
    qk = jnp.einsum('bnhid,bnhjd->bnhij', q, k) * decay
    q_dec = q * jnp.exp(G)[..., None]
    k_dec = k * jnp.exp(G[..., -1:] - G)[..., None]
    g_last = jnp.exp(G[..., -1])

    def step(state, inp):
        u_c, w_c, qk_c, qd_c, kd_c, gl_c = inp
        v_new = u_c - jnp.einsum('bhcd,bhde->bhce', w_c, state)
        o_c = jnp.einsum('bhcd,bhde->bhce', qd_c, state) + jnp.einsum('bhij,bhje->bhie', qk_c, v_new)
        state = state * gl_c[..., None, None] + jnp.einsum('bhcd,bhce->bhde', kd_c, v_new)
        return state, o_c

    xs = (jnp.moveaxis(u, 1, 0), jnp.moveaxis(w, 1, 0), jnp.moveaxis(qk, 1, 0),
          jnp.moveaxis(q_dec, 1, 0), jnp.moveaxis(k_dec, 1, 0), jnp.moveaxis(g_last, 1, 0))
    state0 = jnp.zeros((bsz, nh, dk, dv), jnp.float32)
    _, o = lax.scan(step, state0, xs)
    return o.transpose(1, 0, 3, 2, 4).reshape(bsz, seq, nh, dv)


def gated_deltanet(h, w_in, conv_w, a_log, dt_bias, norm_w, w_out):
    bsz, seq, _ = h.shape
    proj = h @ w_in
    qkv, z, b, a = jnp.split(proj, [GDN_CONV_DIM, GDN_CONV_DIM + GDN_V, GDN_CONV_DIM + GDN_V + GDN_HEADS], axis=-1)
    qkv = jax.nn.silu(causal_dwconv(qkv, conv_w)).astype(jnp.float32)
    q, k, v = jnp.split(qkv, [GDN_QK, 2 * GDN_QK], axis=-1)
    q = l2norm(q.reshape(bsz, seq, GDN_HEADS, GDN_DK))
    k = l2norm(k.reshape(bsz, seq, GDN_HEADS, GDN_DK))
    v = v.reshape(bsz, seq, GDN_HEADS, GDN_DV)
    beta = jax.nn.sigmoid(b.astype(jnp.float32))
    g = -jnp.exp(a_log.astype(jnp.float32)) * jax.nn.softplus(a.astype(jnp.float32) + dt_bias.astype(jnp.float32))
    o = chunk_gated_delta_rule(q, k, v, beta, g)
    o = o * lax.rsqrt(jnp.mean(o * o, axis=-1, keepdims=True) + NORM_EPS) * norm_w.astype(jnp.float32)
    o = o * jax.nn.silu(z.astype(jnp.float32).reshape(bsz, seq, GDN_HEADS, GDN_DV))
    return o.reshape(bsz, seq, GDN_V).astype(h.dtype) @ w_out


def chunked_sgu(h, w_in, ln_g, ln_b, w_s, b_s, w_out):
    bsz, seq, _ = h.shape
    zz = jax.nn.gelu(h @ w_in, approximate=False)
    u, v = jnp.split(zz, 2, axis=-1)
    v = layer_norm(v, ln_g, ln_b)
    n = seq // SGU_CHUNK
    vc = v.reshape(bsz, n, SGU_CHUNK, SGU_GROUPS, SGU_GROUP_DIM)
    tri = jnp.tril(jnp.ones((SGU_CHUNK, SGU_CHUNK), dtype=bool))
    ws = jnp.where(tri, w_s, 0.0).astype(v.dtype)
    mixed = jnp.einsum('gts,bnsgc->bntgc', ws, vc) + b_s.T.astype(v.dtype)[None, None, :, :, None]
    return (u * mixed.reshape(bsz, seq, SGU_WIDTH)) @ w_out


def conv_glu_ffn(h, w_in, conv_w, conv_b, w_out):
    y = causal_dwconv(h @ w_in, conv_w) + conv_b.astype(h.dtype)
    gate, up = jnp.split(y, 2, axis=-1)
    return (jax.nn.silu(gate) * up) @ w_out


def setup_inputs(seed: int = 0) -> dict:
    key = jax.random.key(seed)
    ks = iter(jax.random.split(key, 32))
    f32 = jnp.float32
    n_a = (DEPTH + 1) // 2
    n_b = DEPTH // 2

    def nrm(shape, fan_in):
        return jax.random.normal(next(ks), shape, f32) * fan_in ** -0.5

    def gain(shape):
        return 1.0 + 0.02 * jax.random.normal(next(ks), shape, f32)

    def small(shape):
        return 0.02 * jax.random.normal(next(ks), shape, f32)

    x = jax.random.normal(next(ks), (BATCH, SEQ, D_MODEL), f32)
    p = jax.random.normal(next(ks), (DEPTH, BATCH, SEQ, PLE_DIM), f32)
    mix_norm = gain((DEPTH, D_MODEL))
    gdn_w_in = nrm((n_a, D_MODEL, GDN_IN), D_MODEL)
    gdn_conv_w = nrm((n_a, GDN_CONV, GDN_CONV_DIM), GDN_CONV)
    gdn_a_log = jnp.log(jax.random.uniform(next(ks), (n_a, GDN_HEADS), f32, 1.0, 16.0))
    dt = jnp.exp(jax.random.uniform(next(ks), (n_a, GDN_HEADS), f32, math.log(1e-3), math.log(1e-1)))
    gdn_dt_bias = dt + jnp.log(-jnp.expm1(-dt))
    gdn_norm_w = gain((n_a, GDN_DV))
    gdn_w_out = nrm((n_a, GDN_V, D_MODEL), GDN_V)
    sgu_w_in = nrm((n_b, D_MODEL, 2 * SGU_WIDTH), D_MODEL)
    sgu_ln_g = gain((n_b, SGU_WIDTH))
    sgu_ln_b = small((n_b, SGU_WIDTH))
    sgu_w_s = nrm((n_b, SGU_GROUPS, SGU_CHUNK, SGU_CHUNK), SGU_CHUNK)
    sgu_b_s = gain((n_b, SGU_GROUPS, SGU_CHUNK))
    sgu_w_out = nrm((n_b, SGU_WIDTH, D_MODEL), SGU_WIDTH)
    ffn_norm = gain((DEPTH, D_MODEL))
    ffn_w_in = nrm((DEPTH, D_MODEL, 2 * D_FF), D_MODEL)
    ffn_conv_w = nrm((DEPTH, FFN_CONV, 2 * D_FF), FFN_CONV)
    ffn_conv_b = small((DEPTH, 2 * D_FF))
    ffn_w_out = nrm((DEPTH, D_FF, D_MODEL), D_FF)
    ple_norm = gain((DEPTH, D_MODEL))
    ple_w_gate = nrm((DEPTH, D_MODEL, D_MODEL), D_MODEL)
    ple_w_proj = nrm((DEPTH, PLE_DIM, D_MODEL), PLE_DIM)
    final_norm = gain((D_MODEL,))
    return {'x': x, 'p': p, 'mix_norm': mix_norm,
            'gdn_w_in': gdn_w_in, 'gdn_conv_w': gdn_conv_w, 'gdn_a_log': gdn_a_log,
            'gdn_dt_bias': gdn_dt_bias, 'gdn_norm_w': gdn_norm_w, 'gdn_w_out': gdn_w_out,
            'sgu_w_in': sgu_w_in, 'sgu_ln_g': sgu_ln_g, 'sgu_ln_b': sgu_ln_b,
            'sgu_w_s': sgu_w_s, 'sgu_b_s': sgu_b_s, 'sgu_w_out': sgu_w_out,
            'ffn_norm': ffn_norm, 'ffn_w_in': ffn_w_in, 'ffn_conv_w': ffn_conv_w,
            'ffn_conv_b': ffn_conv_b, 'ffn_w_out': ffn_w_out,
            'ple_norm': ple_norm, 'ple_w_gate': ple_w_gate, 'ple_w_proj': ple_w_proj,
            'final_norm': final_norm}


def reference(x, p, mix_norm, gdn_w_in, gdn_conv_w, gdn_a_log, gdn_dt_bias, gdn_norm_w, gdn_w_out,
              sgu_w_in, sgu_ln_g, sgu_ln_b, sgu_w_s, sgu_b_s, sgu_w_out,
              ffn_norm, ffn_w_in, ffn_conv_w, ffn_conv_b, ffn_w_out,
              ple_norm, ple_w_gate, ple_w_proj, final_norm):
    h = x
    for i in range(DEPTH):
        j = i // N_MIXERS
        hn = rms_norm(h, mix_norm[i])
        if i % N_MIXERS == 0:
            h = h + gated_deltanet(hn, gdn_w_in[j], gdn_conv_w[j], gdn_a_log[j], gdn_dt_bias[j],
                                   gdn_norm_w[j], gdn_w_out[j])
        else:
            h = h + chunked_sgu(hn, sgu_w_in[j], sgu_ln_g[j], sgu_ln_b[j], sgu_w_s[j], sgu_b_s[j],
                                sgu_w_out[j])
        h = h + conv_glu_ffn(rms_norm(h, ffn_norm[i]), ffn_w_in[i], ffn_conv_w[i], ffn_conv_b[i], ffn_w_out[i])
        gate = jax.nn.sigmoid(rms_norm(h, ple_norm[i]) @ ple_w_gate[i])
        h = h + gate * (p[i].astype(h.dtype) @ ple_w_proj[i])
    return rms_norm(h, final_norm)
```

```python
import functools
import math

import jax
import jax.numpy as jnp
from jax import lax
from jax.experimental import pallas as pl
from jax.experimental.pallas import tpu as pltpu

F32 = jnp.float32
BF16 = jnp.bfloat16

NORM_EPS = 1e-6
LN_EPS = 1e-5
GDN_HEADS = 8
GDN_DK = 128
GDN_DV = 128
GDN_CONV = 4
SGU_CHUNK = 128
SGU_GROUPS = 8
FFN_CONV = 3

V7X_VMEM_BYTES = 64 * 1024 * 1024
VMEM_LIMIT_BYTES = V7X_VMEM_BYTES - 8 * 1024 * 1024
SUBLANES = 8
LANES = 128

SEQ_TILE = 512
GDN_CHUNK = 64
GDN_GROUP = 4
COL_TILE = 512
FFN_COL_TILE = 256


def _dot(a, b):
    return jnp.dot(a, b, preferred_element_type=F32)


def _dot_nt(a, b):
    return lax.dot_general(a, b, (((1,), (1,)), ((), ())), preferred_element_type=F32)


def _dot_tn(a, b):
    return lax.dot_general(a, b, (((0,), (0,)), ((), ())), preferred_element_type=F32)


def _sigmoid(x):
    return 1.0 / (1.0 + jnp.exp(-x))


def _rms_norm(x, g):
    return x * lax.rsqrt(jnp.mean(x * x, axis=-1, keepdims=True) + NORM_EPS) * g


def _const_spec(shape):
    zeros = (0,) * len(shape)
    return pl.BlockSpec(shape, lambda *_: zeros, pipeline_mode=pl.Buffered(1))


def _params():
    return pltpu.CompilerParams(
        dimension_semantics=("arbitrary", "arbitrary"), vmem_limit_bytes=VMEM_LIMIT_BYTES)


def _gdn_in_kernel(x_ref, nw_ref, w_ref, wba_ref, cw_ref, alog_ref, dtb_ref,
                   q_ref, k_ref, v_ref, z_ref, g_ref, beta_ref, pbuf, carry, *, ts):
    qk_dim = GDN_HEADS * GDN_DK
    conv_dim = 2 * qk_dim + GDN_HEADS * GDN_DV
    hn = _rms_norm(x_ref[...], nw_ref[...]).astype(BF16)

    ba = _dot_nt(wba_ref[...], hn)
    beta_ref[...] = _sigmoid(ba[:GDN_HEADS])
    g_ref[...] = -jnp.exp(alog_ref[...]) * jax.nn.softplus(ba[GDN_HEADS:] + dtb_ref[...])

    for j in range(GDN_HEADS * GDN_DV // COL_TILE):
        c0 = conv_dim + j * COL_TILE
        z_ref[:, j * COL_TILE:(j + 1) * COL_TILE] = _dot(hn, w_ref[:, c0:c0 + COL_TILE]).astype(BF16)

    @pl.when(pl.program_id(1) == 0)
    def _():
        carry[...] = jnp.zeros_like(carry)

    out_refs = (q_ref, k_ref, v_ref)
    for j in range(conv_dim // COL_TILE):
        cols = slice(j * COL_TILE, (j + 1) * COL_TILE)
        y = _dot(hn, w_ref[:, cols])
        pbuf[0:SUBLANES, :] = carry[:, cols]
        pbuf[SUBLANES:SUBLANES + ts, :] = y
        carry[:, cols] = y[ts - SUBLANES:, :]
        acc = y * cw_ref[GDN_CONV - 1:GDN_CONV, cols]
        for tap in range(GDN_CONV - 1):
            back = GDN_CONV - 1 - tap
            acc = acc + pbuf[SUBLANES - back:SUBLANES - back + ts, :] * cw_ref[tap:tap + 1, cols]
        act = acc * _sigmoid(acc)
        which, local = divmod(j * COL_TILE, qk_dim)
        dst = out_refs[which]
        if which < 2:
            scale = GDN_DK ** -0.5 if which == 0 else 1.0
            for hh in range(COL_TILE // GDN_DK):
                seg = act[:, hh * GDN_DK:(hh + 1) * GDN_DK]
                seg = seg * lax.rsqrt(jnp.sum(seg * seg, axis=-1, keepdims=True) + NORM_EPS)
                if which == 0:
                    seg = seg * scale
                dst[:, local + hh * GDN_DK:local + (hh + 1) * GDN_DK] = seg.astype(BF16)
        else:
            dst[:, local:local + COL_TILE] = act.astype(BF16)


def _gdn_in(x, norm_w, w_main, w_ba_t, conv_w, a_log, dt_bias):
    bsz, seq, d = x.shape
    ts = SEQ_TILE
    hv = GDN_HEADS * GDN_DV
    conv_dim = conv_w.shape[1]
    tok_spec = lambda width: pl.BlockSpec((None, ts, width), lambda b, s: (b, s, 0))
    gate_spec = pl.BlockSpec((None, GDN_HEADS, ts), lambda b, s: (b, 0, s))
    act = jax.ShapeDtypeStruct((bsz, seq, hv), BF16)
    gate = jax.ShapeDtypeStruct((bsz, GDN_HEADS, seq), F32)
    return pl.pallas_call(
        functools.partial(_gdn_in_kernel, ts=ts),
        grid=(bsz, seq // ts),
        in_specs=[tok_spec(d), _const_spec((1, d)), _const_spec(w_main.shape), _const_spec(w_ba_t.shape),
                  _const_spec(conv_w.shape), _const_spec((GDN_HEADS, 1)), _const_spec((GDN_HEADS, 1))],
        out_specs=[tok_spec(hv)] * 4 + [gate_spec] * 2,
        out_shape=[act] * 4 + [gate] * 2,
        scratch_shapes=[pltpu.VMEM((SUBLANES + ts, COL_TILE), F32), pltpu.VMEM((SUBLANES, conv_dim), F32)],
        compiler_params=_params(),
        name="gdn_in",
    )(x, norm_w, w_main, w_ba_t, conv_w, a_log, dt_bias)


def _gdn_core_kernel(q_ref, k_ref, v_ref, z_ref, g_ref, beta_ref, nw_ref, o_ref,
                     gcum, u_s, w_s, qk_s, qd_s, kd_s, gl_s, *, chunk, group):
    c = chunk
    n_chunks = g_ref.shape[0]
    row = lax.broadcasted_iota(jnp.int32, (c, c), 0)
    col = lax.broadcasted_iota(jnp.int32, (c, c), 1)
    lower = row >= col
    strict = row > col
    eye = row == col
    eye_f = jnp.where(eye, 1.0, 0.0).astype(F32)

    upper_ones = jnp.where(row <= col, 1.0, 0.0).astype(BF16)
    g_all = g_ref[...]
    g_hi = g_all.astype(BF16)
    rem = g_all - g_hi.astype(F32)
    g_mid = rem.astype(BF16)
    g_lo = (rem - g_mid.astype(F32)).astype(BF16)
    gcum[...] = _dot(g_hi, upper_ones) + _dot(g_mid, upper_ones) + _dot(g_lo, upper_ones)

    def to_col(r):
        return jnp.sum(jnp.where(eye, r, 0.0), axis=1, keepdims=True)

    def prepare(n):
        c0 = pl.multiple_of(n * c, c)
        rows = pl.ds(c0, c)
        q = q_ref[rows, :]
        k = k_ref[rows, :]
        v = v_ref[rows, :]
        g_row = gcum[pl.ds(n, 1), :]
        g_col = to_col(g_row)
        b_col = to_col(beta_ref[pl.ds(n, 1), :])
        g_last = g_row[:, c - 1:c]
        decay = jnp.where(lower, jnp.exp(g_col - g_row), 0.0)
        kf = k.astype(F32)
        kb = kf * b_col
        a_strict = jnp.where(strict, _dot_nt(kb.astype(BF16), k) * decay, 0.0)
        t_mat = eye_f - a_strict
        power = a_strict
        span = 2
        while span < c:
            pb = power.astype(BF16)
            power = _dot(pb, pb)
            t_mat = t_mat + _dot(t_mat.astype(BF16), power.astype(BF16))
            span *= 2
        e_g = jnp.exp(g_col)
        rhs = jnp.concatenate([(v.astype(F32) * b_col).astype(BF16), (kb * e_g).astype(BF16)], axis=1)
        uw = _dot(t_mat.astype(BF16), rhs)
        u_s[rows, :] = uw[:, :GDN_DV]
        w_s[rows, :] = uw[:, GDN_DV:].astype(BF16)
        qk_s[rows, :] = (_dot_nt(q, k) * decay).astype(BF16)
        qd_s[rows, :] = (q.astype(F32) * e_g).astype(BF16)
        kd_s[rows, :] = (kf * jnp.exp(g_last - g_col)).astype(BF16)
        gl_s[pl.ds(n, 1), :] = jnp.broadcast_to(jnp.exp(g_last), (1, GDN_DV))

    def prepare_group(i, carry):
        for j in range(group):
            prepare(i * group + j)
        return carry

    lax.fori_loop(0, n_chunks // group, prepare_group, 0)

    nw = nw_ref[...]

    def step(n, state):
        c0 = pl.multiple_of(n * c, c)
        rows = pl.ds(c0, c)
        sb = state.astype(BF16)
        v_new = u_s[rows, :] - _dot(w_s[rows, :], sb)
        vb = v_new.astype(BF16)
        o = _dot(qd_s[rows, :], sb) + _dot(qk_s[rows, :], vb)
        state = state * gl_s[pl.ds(n, 1), :] + _dot_tn(kd_s[rows, :], vb)
        o = o * lax.rsqrt(jnp.mean(o * o, axis=-1, keepdims=True) + NORM_EPS) * nw
        zz = z_ref[rows, :].astype(F32)
        o_ref[rows, :] = (o * (zz * _sigmoid(zz))).astype(BF16)
        return state

    lax.fori_loop(0, n_chunks, step, jnp.zeros((GDN_DK, GDN_DV), F32), unroll=2)


def _gdn_core(q, k, v, z, g, beta, norm_w):
    bsz, seq, _ = q.shape
    c = GDN_CHUNK
    n_chunks = seq // c
    g = g.reshape(bsz, GDN_HEADS, n_chunks, c)
    beta = beta.reshape(bsz, GDN_HEADS, n_chunks, c)
    head_spec = pl.BlockSpec((None, seq, GDN_DK), lambda b, h: (b, 0, h))
    gate_spec = pl.BlockSpec((None, None, n_chunks, c), lambda b, h: (b, h, 0, 0))
    return pl.pallas_call(
        functools.partial(_gdn_core_kernel, chunk=c, group=GDN_GROUP),
        grid=(bsz, GDN_HEADS),
        in_specs=[head_spec] * 4 + [gate_spec] * 2 + [_const_spec((1, GDN_DV))],
        out_specs=head_spec,
        out_shape=jax.ShapeDtypeStruct((bsz, seq, GDN_HEADS * GDN_DV), BF16),
        scratch_shapes=[pltpu.VMEM((n_chunks, c), F32),
                        pltpu.VMEM((seq, GDN_DV), F32), pltpu.VMEM((seq, GDN_DK), BF16),
                        pltpu.VMEM((seq, c), BF16), pltpu.VMEM((seq, GDN_DK), BF16),
                        pltpu.VMEM((seq, GDN_DK), BF16), pltpu.VMEM((n_chunks, GDN_DV), F32)],
        compiler_params=_params(),
        name="gdn_core",
    )(q, k, v, z, g, beta, norm_w)


def _chan_mix_kernel(res_ref, mix_ref, wmix_ref, fnw_ref, win_ref, cw_ref, cb_ref, wout_ref,
                     pnw_ref, wgate_ref, p_ref, wproj_ref, onw_ref, out_ref, pbuf, carry, acc,
                     *, ts, d_ff, final_norm):
    h1 = res_ref[...] + _dot(mix_ref[...], wmix_ref[...])
    hn = _rms_norm(h1, fnw_ref[...]).astype(BF16)
    acc[...] = h1

    @pl.when(pl.program_id(1) == 0)
    def _():
        carry[...] = jnp.zeros_like(carry)

    def conv(cols):
        y = _dot(hn, win_ref[:, cols])
        pbuf[0:SUBLANES, :] = carry[:, cols]
        pbuf[SUBLANES:SUBLANES + ts, :] = y
        carry[:, cols] = y[ts - SUBLANES:, :]
        out = y * cw_ref[FFN_CONV - 1:FFN_CONV, cols] + cb_ref[:, cols]
        for tap in range(FFN_CONV - 1):
            back = FFN_CONV - 1 - tap
            out = out + pbuf[SUBLANES - back:SUBLANES - back + ts, :] * cw_ref[tap:tap + 1, cols]
        return out

    ct = FFN_COL_TILE
    for j in range(d_ff // ct):
        gate = conv(slice(j * ct, (j + 1) * ct))
        up = conv(slice(d_ff + j * ct, d_ff + (j + 1) * ct))
        hidden = (gate * _sigmoid(gate) * up).astype(BF16)
        acc[...] += _dot(hidden, wout_ref[j * ct:(j + 1) * ct, :])

    h2 = acc[...]
    gate = _sigmoid(_dot(_rms_norm(h2, pnw_ref[...]).astype(BF16), wgate_ref[...]))
    h3 = h2 + gate * _dot(p_ref[...].astype(BF16), wproj_ref[...])
    if final_norm:
        h3 = _rms_norm(h3, onw_ref[...])
    out_ref[...] = h3


def _chan_mix(res, mix, w_mix, ffn_norm, w_in, conv_w, conv_b, w_out, ple_norm, w_gate, p, w_proj,
              out_norm, final_norm):
    bsz, seq, d = res.shape
    ts = SEQ_TILE
    d_ff = w_out.shape[0]
    tok_spec = lambda width: pl.BlockSpec((None, ts, width), lambda b, s: (b, s, 0))
    consts = (w_mix, ffn_norm, w_in, conv_w, conv_b, w_out, ple_norm, w_gate)
    return pl.pallas_call(
        functools.partial(_chan_mix_kernel, ts=ts, d_ff=d_ff, final_norm=final_norm),
        grid=(bsz, seq // ts),
        in_specs=[tok_spec(d), tok_spec(mix.shape[-1])] + [_const_spec(a.shape) for a in consts]
                 + [tok_spec(p.shape[-1]), _const_spec(w_proj.shape), _const_spec(out_norm.shape)],
        out_specs=tok_spec(d),
        out_shape=jax.ShapeDtypeStruct((bsz, seq, d), F32),
        scratch_shapes=[pltpu.VMEM((SUBLANES + ts, FFN_COL_TILE), F32),
                        pltpu.VMEM((SUBLANES, 2 * d_ff), F32), pltpu.VMEM((ts, d), F32)],
        compiler_params=_params(),
        name="chan_mix",
    )(res, mix, *consts, p, w_proj, out_norm)


def _gelu(x):
    return 0.5 * x * (1.0 + lax.erf(x * (1.0 / math.sqrt(2.0))))


def _sgu_kernel(h_ref, nw_ref, win_ref, lng_ref, lnb_ref, ws_ref, bs_ref, out_ref, vbuf, *, ts, width):
    hn = _rms_norm(h_ref[...], nw_ref[...]).astype(BF16)
    n_col = width // COL_TILE
    for j in range(n_col):
        cols = slice(j * COL_TILE, (j + 1) * COL_TILE)
        vbuf[:, cols] = _gelu(_dot(hn, win_ref[:, width + j * COL_TILE:width + (j + 1) * COL_TILE]))
    v = vbuf[...]
    mu = jnp.mean(v, axis=-1, keepdims=True)
    vc = v - mu
    inv = lax.rsqrt(jnp.mean(vc * vc, axis=-1, keepdims=True) + LN_EPS)
    vbuf[...] = vc * inv * lng_ref[...] + lnb_ref[...]

    row = lax.broadcasted_iota(jnp.int32, (SGU_CHUNK, SGU_CHUNK), 0)
    col = lax.broadcasted_iota(jnp.int32, (SGU_CHUNK, SGU_CHUNK), 1)
    tri = row >= col
    gd = width // SGU_GROUPS
    for grp in range(SGU_GROUPS):
        cols = slice(grp * gd, (grp + 1) * gd)
        w_tri = jnp.where(tri, ws_ref[grp], 0.0).astype(BF16)
        bias = bs_ref[:, grp:grp + 1]
        u = _gelu(_dot(hn, win_ref[:, cols]))
        for i in range(ts // SGU_CHUNK):
            rows = slice(i * SGU_CHUNK, (i + 1) * SGU_CHUNK)
            mixed = _dot(w_tri, vbuf[rows, cols].astype(BF16)) + bias
            out_ref[rows, cols] = (u[rows, :] * mixed).astype(BF16)


def _sgu_mix(h, norm_w, w_in, ln_g, ln_b, w_s, b_s_t):
    bsz, seq, d = h.shape
    ts = SEQ_TILE
    width = w_in.shape[1] // 2
    tok_spec = lambda w: pl.BlockSpec((None, ts, w), lambda b, s: (b, s, 0))
    consts = (norm_w, w_in, ln_g, ln_b, w_s, b_s_t)
    return pl.pallas_call(
        functools.partial(_sgu_kernel, ts=ts, width=width),
        grid=(bsz, seq // ts),
        in_specs=[tok_spec(d)] + [_const_spec(a.shape) for a in consts],
        out_specs=tok_spec(width),
        out_shape=jax.ShapeDtypeStruct((bsz, seq, width), BF16),
        scratch_shapes=[pltpu.VMEM((ts, width), F32)],
        compiler_params=_params(),
        name="sgu_mix",
    )(h, *consts)


def kernel(x, p, mix_norm, gdn_w_in, gdn_conv_w, gdn_a_log, gdn_dt_bias, gdn_norm_w, gdn_w_out,
           sgu_w_in, sgu_ln_g, sgu_ln_b, sgu_w_s, sgu_b_s, sgu_w_out,
           ffn_norm, ffn_w_in, ffn_conv_w, ffn_conv_b, ffn_w_out,
           ple_norm, ple_w_gate, ple_w_proj, final_norm):
    row = lambda a: a.reshape(1, -1).astype(F32)
    bf = lambda a: a.astype(BF16)
    hv = GDN_HEADS * GDN_DV
    conv_dim = gdn_conv_w.shape[-1]

    def chan_mix(layer, res, mix, w_mix, final):
        return _chan_mix(res, mix, bf(w_mix), row(ffn_norm[layer]), bf(ffn_w_in[layer]), ffn_conv_w[layer],
                         row(ffn_conv_b[layer]), bf(ffn_w_out[layer]), row(ple_norm[layer]),
                         bf(ple_w_gate[layer]), p[layer], bf(ple_w_proj[layer]), row(final_norm), final)

    w_in = gdn_w_in[0]
    w_main = bf(w_in[:, :conv_dim + hv])
    w_ba_t = bf(w_in[:, conv_dim + hv:].T)
    q, k, v, z, g, beta = _gdn_in(x, row(mix_norm[0]), w_main, w_ba_t, gdn_conv_w[0],
                                  gdn_a_log[0].reshape(-1, 1), gdn_dt_bias[0].reshape(-1, 1))
    o = _gdn_core(q, k, v, z, g, beta, row(gdn_norm_w[0]))
    h = chan_mix(0, x, o, gdn_w_out[0], False)

    um = _sgu_mix(h, row(mix_norm[1]), bf(sgu_w_in[0]), row(sgu_ln_g[0]), row(sgu_ln_b[0]),
                  sgu_w_s[0], sgu_b_s[0].T)
    return chan_mix(1, h, um, sgu_w_out[0], True)
```

```python
import functools
import math

import jax
import jax.numpy as jnp
from jax import lax
from jax.experimental import pallas as pl
from jax.experimental.pallas import tpu as pltpu

F32 = jnp.float32
BF16 = jnp.bfloat16

NORM_EPS = 1e-6
LN_EPS = 1e-5
GDN_HEADS = 8
GDN_DK = 128
GDN_DV = 128
GDN_CONV = 4
SGU_CHUNK = 128
SGU_GROUPS = 8
FFN_CONV = 3

V7X_VMEM_BYTES = 64 * 1024 * 1024
VMEM_LIMIT_BYTES = V7X_VMEM_BYTES - 8 * 1024 * 1024
SUBLANES = 8
LANES = 128

SEQ_TILE = 512
GDN_CHUNK = 64
GDN_BATCH_BLOCK = 2
GDN_SEQ_BLOCK = 512
COL_TILE = 512
FFN_COL_TILE = 256


def _dot(a, b):
    return jnp.dot(a, b, preferred_element_type=F32)


def _dot_nt(a, b):
    return lax.dot_general(a, b, (((1,), (1,)), ((), ())), preferred_element_type=F32)


def _dot_tn(a, b):
    return lax.dot_general(a, b, (((0,), (0,)), ((), ())), preferred_element_type=F32)


def _sigmoid(x):
    return 1.0 / (1.0 + jnp.exp(-x))


def _rms_norm(x, g):
    return x * lax.rsqrt(jnp.mean(x * x, axis=-1, keepdims=True) + NORM_EPS) * g


def _const_spec(shape):
    zeros = (0,) * len(shape)
    return pl.BlockSpec(shape, lambda *_: zeros, pipeline_mode=pl.Buffered(1))


def _params():
    return pltpu.CompilerParams(
        dimension_semantics=("arbitrary", "arbitrary"), vmem_limit_bytes=VMEM_LIMIT_BYTES)


def _gdn_in_kernel(x_ref, nw_ref, w_ref, wba_ref, cw_ref, alog_ref, dtb_ref,
                   q_ref, k_ref, v_ref, z_ref, g_ref, beta_ref, pbuf, carry, *, ts):
    qk_dim = GDN_HEADS * GDN_DK
    conv_dim = 2 * qk_dim + GDN_HEADS * GDN_DV
    hn = _rms_norm(x_ref[...], nw_ref[...]).astype(BF16)

    ba = _dot_nt(wba_ref[...], hn)
    beta_ref[...] = _sigmoid(ba[:GDN_HEADS])
    g_ref[...] = -jnp.exp(alog_ref[...]) * jax.nn.softplus(ba[GDN_HEADS:] + dtb_ref[...])

    for j in range(GDN_HEADS * GDN_DV // COL_TILE):
        c0 = conv_dim + j * COL_TILE
        z_ref[:, j * COL_TILE:(j + 1) * COL_TILE] = _dot(hn, w_ref[:, c0:c0 + COL_TILE]).astype(BF16)

    @pl.when(pl.program_id(1) == 0)
    def _():
        carry[...] = jnp.zeros_like(carry)

    out_refs = (q_ref, k_ref, v_ref)
    for j in range(conv_dim // COL_TILE):
        cols = slice(j * COL_TILE, (j + 1) * COL_TILE)
        y = _dot(hn, w_ref[:, cols])
        slot = j % 2
        pbuf[slot, 0:SUBLANES, :] = carry[:, cols]
        pbuf[slot, SUBLANES:SUBLANES + ts, :] = y
        carry[:, cols] = y[ts - SUBLANES:, :]
        acc = y * cw_ref[GDN_CONV - 1:GDN_CONV, cols]
        for tap in range(GDN_CONV - 1):
            back = GDN_CONV - 1 - tap
            acc = acc + pbuf[slot, SUBLANES - back:SUBLANES - back + ts, :] * cw_ref[tap:tap + 1, cols]
        act = acc * _sigmoid(acc)
        which, local = divmod(j * COL_TILE, qk_dim)
        dst = out_refs[which]
        if which < 2:
            scale = GDN_DK ** -0.5 if which == 0 else 1.0
            for hh in range(COL_TILE // GDN_DK):
                seg = act[:, hh * GDN_DK:(hh + 1) * GDN_DK]
                seg = seg * lax.rsqrt(jnp.sum(seg * seg, axis=-1, keepdims=True) + NORM_EPS)
                if which == 0:
                    seg = seg * scale
                dst[:, local + hh * GDN_DK:local + (hh + 1) * GDN_DK] = seg.astype(BF16)
        else:
            dst[:, local:local + COL_TILE] = act.astype(BF16)


def _gdn_in(x, norm_w, w_main, w_ba_t, conv_w, a_log, dt_bias):
    bsz, seq, d = x.shape
    ts = SEQ_TILE
    hv = GDN_HEADS * GDN_DV
    conv_dim = conv_w.shape[1]
    tok_spec = lambda width: pl.BlockSpec((None, ts, width), lambda b, s: (b, s, 0))
    gate_spec = pl.BlockSpec((None, GDN_HEADS, ts), lambda b, s: (b, 0, s))
    act = jax.ShapeDtypeStruct((bsz, seq, hv), BF16)
    gate = jax.ShapeDtypeStruct((bsz, GDN_HEADS, seq), F32)
    return pl.pallas_call(
        functools.partial(_gdn_in_kernel, ts=ts),
        grid=(bsz, seq // ts),
        in_specs=[tok_spec(d), _const_spec((1, d)), _const_spec(w_main.shape), _const_spec(w_ba_t.shape),
                  _const_spec(conv_w.shape), _const_spec((GDN_HEADS, 1)), _const_spec((GDN_HEADS, 1))],
        out_specs=[tok_spec(hv)] * 4 + [gate_spec] * 2,
        out_shape=[act] * 4 + [gate] * 2,
        scratch_shapes=[pltpu.VMEM((2, SUBLANES + ts, COL_TILE), F32), pltpu.VMEM((SUBLANES, conv_dim), F32)],
        compiler_params=_params(),
        name="gdn_in",
    )(x, norm_w, w_main, w_ba_t, conv_w, a_log, dt_bias)


def _gdn_core_kernel(q_ref, k_ref, v_ref, z_ref, g_ref, beta_ref, nw_ref, o_ref,
                     gcum, u_s, w_s, qk_s, qd_s, kd_s, gl_s, state, *, chunk):
    c = chunk
    nb, nh, n_chunks, _ = g_ref.shape
    row = lax.broadcasted_iota(jnp.int32, (c, c), 0)
    col = lax.broadcasted_iota(jnp.int32, (c, c), 1)
    lower = row >= col
    strict = row > col
    eye = row == col
    eye_f = jnp.where(eye, 1.0, 0.0).astype(F32)

    @pl.when(pl.program_id(1) == 0)
    def _():
        state[...] = jnp.zeros_like(state)

    upper_ones = jnp.where(row <= col, 1.0, 0.0).astype(BF16)
    g_all = g_ref[...].reshape(nb * nh * n_chunks, c)
    g_hi = g_all.astype(BF16)
    rem = g_all - g_hi.astype(F32)
    g_mid = rem.astype(BF16)
    g_lo = (rem - g_mid.astype(F32)).astype(BF16)
    gcum[...] = _dot(g_hi, upper_ones) + _dot(g_mid, upper_ones) + _dot(g_lo, upper_ones)

    def to_col(r):
        return jnp.sum(jnp.where(eye, r, 0.0), axis=1, keepdims=True)

    seqs = [(b, h) for b in range(nb) for h in range(nh)]

    def prepare(n, carry):
        rows = pl.ds(pl.multiple_of(n * c, c), c)
        a_list, rhs_list = [], []
        for b, h in seqs:
            seq_id = b * nh + h
            lanes = slice(h * GDN_DK, (h + 1) * GDN_DK)
            q = q_ref[b, rows, lanes]
            k = k_ref[b, rows, lanes]
            v = v_ref[b, rows, lanes]
            g_row = gcum[pl.ds(seq_id * n_chunks + n, 1), :]
            g_col = to_col(g_row)
            b_col = to_col(beta_ref[b, h, pl.ds(n, 1), :])
            g_last = g_row[:, c - 1:c]
            decay = jnp.where(lower, jnp.exp(g_col - g_row), 0.0)
            kf = k.astype(F32)
            kb = kf * b_col
            e_g = jnp.exp(g_col)
            a_list.append(jnp.where(strict, _dot_nt(kb.astype(BF16), k) * decay, 0.0))
            rhs_list.append(jnp.concatenate(
                [(v.astype(F32) * b_col).astype(BF16), (kb * e_g).astype(BF16)], axis=1))
            qk_s[seq_id, rows, :] = (_dot_nt(q, k) * decay).astype(BF16)
            qd_s[seq_id, rows, :] = (q.astype(F32) * e_g).astype(BF16)
            kd_s[seq_id, rows, :] = (kf * jnp.exp(g_last - g_col)).astype(BF16)
            gl_s[seq_id, pl.ds(n, 1), :] = jnp.broadcast_to(jnp.exp(g_last), (1, GDN_DV))
        t_list = [eye_f - a for a in a_list]
        p_list = [a.astype(BF16) for a in a_list]
        span = 2
        while span < c:
            p_list = [_dot(p, p).astype(BF16) for p in p_list]
            t_list = [t + _dot(t.astype(BF16), p) for t, p in zip(t_list, p_list)]
            span *= 2
        for (b, h), t, rhs in zip(seqs, t_list, rhs_list):
            seq_id = b * nh + h
            uw = _dot(t.astype(BF16), rhs)
            u_s[seq_id, rows, :] = uw[:, :GDN_DV]
            w_s[seq_id, rows, :] = uw[:, GDN_DV:].astype(BF16)
        return carry

    nw = nw_ref[...]

    def step(n, carry):
        rows = pl.ds(pl.multiple_of(n * c, c), c)
        st_list, vb_list, o_list = [], [], []
        for b, h in seqs:
            seq_id = b * nh + h
            st = state[seq_id]
            sb = st.astype(BF16)
            vb_list.append((u_s[seq_id, rows, :] - _dot(w_s[seq_id, rows, :], sb)).astype(BF16))
            o_list.append(_dot(qd_s[seq_id, rows, :], sb))
            st_list.append(st)
        for (b, h), st, vb, o in zip(seqs, st_list, vb_list, o_list):
            seq_id = b * nh + h
            lanes = slice(h * GDN_DV, (h + 1) * GDN_DV)
            o = o + _dot(qk_s[seq_id, rows, :], vb)
            state[seq_id] = st * gl_s[seq_id, pl.ds(n, 1), :] + _dot_tn(kd_s[seq_id, rows, :], vb)
            o = o * lax.rsqrt(jnp.mean(o * o, axis=-1, keepdims=True) + NORM_EPS) * nw
            zz = z_ref[b, rows, lanes].astype(F32)
            o_ref[b, rows, lanes] = (o * (zz * _sigmoid(zz))).astype(BF16)
        return carry

    lax.fori_loop(0, n_chunks, prepare, 0)
    lax.fori_loop(0, n_chunks, step, 0)


def _gdn_core(q, k, v, z, g, beta, norm_w):
    bsz, seq, width = q.shape
    c = GDN_CHUNK
    nb, sb = GDN_BATCH_BLOCK, GDN_SEQ_BLOCK
    n_chunks = sb // c
    n_seq = nb * GDN_HEADS
    g = g.reshape(bsz, GDN_HEADS, seq // c, c)
    beta = beta.reshape(bsz, GDN_HEADS, seq // c, c)
    tok_spec = pl.BlockSpec((nb, sb, width), lambda b, s: (b, s, 0))
    gate_spec = pl.BlockSpec((nb, GDN_HEADS, n_chunks, c), lambda b, s: (b, 0, s, 0))
    return pl.pallas_call(
        functools.partial(_gdn_core_kernel, chunk=c),
        grid=(bsz // nb, seq // sb),
        in_specs=[tok_spec] * 4 + [gate_spec] * 2 + [_const_spec((1, GDN_DV))],
        out_specs=tok_spec,
        out_shape=jax.ShapeDtypeStruct((bsz, seq, width), BF16),
        scratch_shapes=[pltpu.VMEM((n_seq * n_chunks, c), F32),
                        pltpu.VMEM((n_seq, sb, GDN_DV), F32), pltpu.VMEM((n_seq, sb, GDN_DK), BF16),
                        pltpu.VMEM((n_seq, sb, c), BF16), pltpu.VMEM((n_seq, sb, GDN_DK), BF16),
                        pltpu.VMEM((n_seq, sb, GDN_DK), BF16), pltpu.VMEM((n_seq, n_chunks, GDN_DV), F32),
                        pltpu.VMEM((n_seq, GDN_DK, GDN_DV), F32)],
        compiler_params=_params(),
        name="gdn_core",
    )(q, k, v, z, g, beta, norm_w)


def _chan_mix_kernel(res_ref, mix_ref, wmix_ref, fnw_ref, win_ref, cw_ref, cb_ref, wout_ref,
                     pnw_ref, wgate_ref, p_ref, wproj_ref, onw_ref, out_ref, pbuf, carry, acc,
                     *, ts, d_ff, final_norm):
    h1 = res_ref[...] + _dot(mix_ref[...], wmix_ref[...])
    ple = _dot(p_ref[...].astype(BF16), wproj_ref[...])
    hn = _rms_norm(h1, fnw_ref[...]).astype(BF16)
    acc[...] = h1

    @pl.when(pl.program_id(1) == 0)
    def _():
        carry[...] = jnp.zeros_like(carry)

    ct = FFN_COL_TILE

    def project(j):
        return tuple(_dot(hn, win_ref[:, c0:c0 + ct]) for c0 in (j * ct, d_ff + j * ct))

    def conv(y, c0, slot):
        cols = slice(c0, c0 + ct)
        pbuf[slot, 0:SUBLANES, :] = carry[:, cols]
        pbuf[slot, SUBLANES:SUBLANES + ts, :] = y
        carry[:, cols] = y[ts - SUBLANES:, :]
        out = y * cw_ref[FFN_CONV - 1:FFN_CONV, cols] + cb_ref[:, cols]
        for tap in range(FFN_CONV - 1):
            back = FFN_CONV - 1 - tap
            out = out + pbuf[slot, SUBLANES - back:SUBLANES - back + ts, :] * cw_ref[tap:tap + 1, cols]
        return out

    n_tiles = d_ff // ct
    ys = project(0)
    for j in range(n_tiles):
        gate = conv(ys[0], j * ct, 2 * (j % 2))
        up = conv(ys[1], d_ff + j * ct, 2 * (j % 2) + 1)
        hidden = (gate * _sigmoid(gate) * up).astype(BF16)
        if j + 1 < n_tiles:
            ys = project(j + 1)
        acc[...] += _dot(hidden, wout_ref[j * ct:(j + 1) * ct, :])

    h2 = acc[...]
    gate = _sigmoid(_dot(_rms_norm(h2, pnw_ref[...]).astype(BF16), wgate_ref[...]))
    h3 = h2 + gate * ple
    if final_norm:
        h3 = _rms_norm(h3, onw_ref[...])
    out_ref[...] = h3


def _chan_mix(res, mix, w_mix, ffn_norm, w_in, conv_w, conv_b, w_out, ple_norm, w_gate, p, w_proj,
              out_norm, final_norm):
    bsz, seq, d = res.shape
    ts = SEQ_TILE
    d_ff = w_out.shape[0]
    tok_spec = lambda width: pl.BlockSpec((None, ts, width), lambda b, s: (b, s, 0))
    consts = (w_mix, ffn_norm, w_in, conv_w, conv_b, w_out, ple_norm, w_gate)
    return pl.pallas_call(
        functools.partial(_chan_mix_kernel, ts=ts, d_ff=d_ff, final_norm=final_norm),
        grid=(bsz, seq // ts),
        in_specs=[tok_spec(d), tok_spec(mix.shape[-1])] + [_const_spec(a.shape) for a in consts]
                 + [tok_spec(p.shape[-1]), _const_spec(w_proj.shape), _const_spec(out_norm.shape)],
        out_specs=tok_spec(d),
        out_shape=jax.ShapeDtypeStruct((bsz, seq, d), F32),
        scratch_shapes=[pltpu.VMEM((4, SUBLANES + ts, FFN_COL_TILE), F32),
                        pltpu.VMEM((SUBLANES, 2 * d_ff), F32), pltpu.VMEM((ts, d), F32)],
        compiler_params=_params(),
        name="chan_mix",
    )(res, mix, *consts, p, w_proj, out_norm)


def _gelu(x):
    return 0.5 * x * (1.0 + lax.erf(x * (1.0 / math.sqrt(2.0))))


def _sgu_kernel(h_ref, nw_ref, win_ref, lng_ref, lnb_ref, ws_ref, bs_ref, out_ref, vbuf, *, ts, width):
    hn = _rms_norm(h_ref[...], nw_ref[...]).astype(BF16)
    n_col = width // COL_TILE
    for j in range(n_col):
        cols = slice(j * COL_TILE, (j + 1) * COL_TILE)
        vbuf[:, cols] = _gelu(_dot(hn, win_ref[:, width + j * COL_TILE:width + (j + 1) * COL_TILE]))
    v = vbuf[...]
    mu = jnp.mean(v, axis=-1, keepdims=True)
    vc = v - mu
    inv = lax.rsqrt(jnp.mean(vc * vc, axis=-1, keepdims=True) + LN_EPS)
    vbuf[...] = vc * inv * lng_ref[...] + lnb_ref[...]

    row = lax.broadcasted_iota(jnp.int32, (SGU_CHUNK, SGU_CHUNK), 0)
    col = lax.broadcasted_iota(jnp.int32, (SGU_CHUNK, SGU_CHUNK), 1)
    tri = row >= col
    gd = width // SGU_GROUPS
    for grp in range(SGU_GROUPS):
        cols = slice(grp * gd, (grp + 1) * gd)
        w_tri = jnp.where(tri, ws_ref[grp], 0.0).astype(BF16)
        bias = bs_ref[:, grp:grp + 1]
        u = _gelu(_dot(hn, win_ref[:, cols]))
        for i in range(ts // SGU_CHUNK):
            rows = slice(i * SGU_CHUNK, (i + 1) * SGU_CHUNK)
            mixed = _dot(w_tri, vbuf[rows, cols].astype(BF16)) + bias
            out_ref[rows, cols] = (u[rows, :] * mixed).astype(BF16)


def _sgu_mix(h, norm_w, w_in, ln_g, ln_b, w_s, b_s_t):
    bsz, seq, d = h.shape
    ts = SEQ_TILE
    width = w_in.shape[1] // 2
    tok_spec = lambda w: pl.BlockSpec((None, ts, w), lambda b, s: (b, s, 0))
    consts = (norm_w, w_in, ln_g, ln_b, w_s, b_s_t)
    return pl.pallas_call(
        functools.partial(_sgu_kernel, ts=ts, width=width),
        grid=(bsz, seq // ts),
        in_specs=[tok_spec(d)] + [_const_spec(a.shape) for a in consts],
        out_specs=tok_spec(width),
        out_shape=jax.ShapeDtypeStruct((bsz, seq, width), BF16),
        scratch_shapes=[pltpu.VMEM((ts, width), F32)],
        compiler_params=_params(),
        name="sgu_mix",
    )(h, *consts)


def kernel(x, p, mix_norm, gdn_w_in, gdn_conv_w, gdn_a_log, gdn_dt_bias, gdn_norm_w, gdn_w_out,
           sgu_w_in, sgu_ln_g, sgu_ln_b, sgu_w_s, sgu_b_s, sgu_w_out,
           ffn_norm, ffn_w_in, ffn_conv_w, ffn_conv_b, ffn_w_out,
           ple_norm, ple_w_gate, ple_w_proj, final_norm):
    row = lambda a: a.reshape(1, -1).astype(F32)
    bf = lambda a: a.astype(BF16)
    hv = GDN_HEADS * GDN_DV
    conv_dim = gdn_conv_w.shape[-1]

    def chan_mix(layer, res, mix, w_mix, final):
        return _chan_mix(res, mix, bf(w_mix), row(ffn_norm[layer]), bf(ffn_w_in[layer]), ffn_conv_w[layer],
                         row(ffn_conv_b[layer]), bf(ffn_w_out[layer]), row(ple_norm[layer]),
                         bf(ple_w_gate[layer]), p[layer], bf(ple_w_proj[layer]), row(final_norm), final)

    w_in = gdn_w_in[0]
    w_main = bf(w_in[:, :conv_dim + hv])
    w_ba_t = bf(w_in[:, conv_dim + hv:].T)
    q, k, v, z, g, beta = _gdn_in(x, row(mix_norm[0]), w_main, w_ba_t, gdn_conv_w[0],
                                  gdn_a_log[0].reshape(-1, 1), gdn_dt_bias[0].reshape(-1, 1))
    o = _gdn_core(q, k, v, z, g, beta, row(gdn_norm_w[0]))
    h = chan_mix(0, x, o, gdn_w_out[0], False)

    um = _sgu_mix(h, row(mix_norm[1]), bf(sgu_w_in[0]), row(sgu_ln_g[0]), row(sgu_ln_b[0]),
                  sgu_w_s[0], sgu_b_s[0].T)
    return chan_mix(1, h, um, sgu_w_out[0], True)
```

```python
import functools
import math

import jax
import jax.numpy as jnp
from jax import lax
from jax.experimental import pallas as pl
from jax.experimental.pallas import tpu as pltpu

F32 = jnp.float32
BF16 = jnp.bfloat16

NORM_EPS = 1e-6
LN_EPS = 1e-5
GDN_HEADS = 8
GDN_DK = 128
GDN_DV = 128
GDN_CONV = 4
SGU_CHUNK = 128
SGU_GROUPS = 8
FFN_CONV = 3

V7X_VMEM_BYTES = 64 * 1024 * 1024
VMEM_LIMIT_BYTES = V7X_VMEM_BYTES - 8 * 1024 * 1024
SUBLANES = 8
LANES = 128

SEQ_TILE = 512
GDN_CHUNK = 64
GDN_BATCH_BLOCK = 2
GDN_SEQ_BLOCK = 512
COL_TILE = 256
SGU_COL_TILE = 512
FFN_COL_TILE = 256
OUT_COL_TILE = 512


def _dot(a, b):
    return jnp.dot(a, b, preferred_element_type=F32)


def _dot_nt(a, b):
    return lax.dot_general(a, b, (((1,), (1,)), ((), ())), preferred_element_type=F32)


def _sigmoid(x):
    return 1.0 / (1.0 + jnp.exp(-x))


def _rms_norm(x, g):
    return x * lax.rsqrt(jnp.mean(x * x, axis=-1, keepdims=True) + NORM_EPS) * g


def _const_spec(shape):
    zeros = (0,) * len(shape)
    return pl.BlockSpec(shape, lambda *_: zeros, pipeline_mode=pl.Buffered(1))


def _params():
    return pltpu.CompilerParams(
        dimension_semantics=("arbitrary", "arbitrary"), vmem_limit_bytes=VMEM_LIMIT_BYTES)


def _gdn_in_kernel(x_ref, nw_ref, w_ref, wba_ref, cw_ref, alog_ref, dtb_ref,
                   q_ref, k_ref, v_ref, z_ref, g_ref, beta_ref, pbuf, carry, *, ts):
    qk_dim = GDN_HEADS * GDN_DK
    conv_dim = 2 * qk_dim + GDN_HEADS * GDN_DV
    hn = _rms_norm(x_ref[...], nw_ref[...]).astype(BF16)

    ba = _dot_nt(wba_ref[...], hn)
    beta_ref[...] = _sigmoid(ba[:GDN_HEADS])
    g_ref[...] = -jnp.exp(alog_ref[...]) * jax.nn.softplus(ba[GDN_HEADS:] + dtb_ref[...])

    @pl.when(pl.program_id(1) == 0)
    def _():
        carry[...] = jnp.zeros_like(carry)

    out_refs = (q_ref, k_ref, v_ref)

    def finish_z(j, y):
        z_ref[:, j * COL_TILE:(j + 1) * COL_TILE] = y.astype(BF16)

    def finish_conv(j, y):
        which, local = divmod(j * COL_TILE, qk_dim)
        dst = out_refs[which]
        for sub in range(COL_TILE // LANES):
            cols = slice(j * COL_TILE + sub * LANES, j * COL_TILE + (sub + 1) * LANES)
            y_sub = y[:, sub * LANES:(sub + 1) * LANES]
            stage = pbuf.at[j % 2, sub]
            stage[0:SUBLANES, :] = carry[:, cols]
            stage[SUBLANES:SUBLANES + ts, :] = y_sub
            carry[:, cols] = y_sub[ts - SUBLANES:, :]
            acc = stage[SUBLANES - (GDN_CONV - 1):SUBLANES - (GDN_CONV - 1) + ts, :] * cw_ref[0:1, cols]
            for tap in range(1, GDN_CONV):
                back = GDN_CONV - 1 - tap
                acc = acc + stage[SUBLANES - back:SUBLANES - back + ts, :] * cw_ref[tap:tap + 1, cols]
            act = acc * _sigmoid(acc)
            if which < 2:
                act = act * lax.rsqrt(jnp.sum(act * act, axis=-1, keepdims=True) + NORM_EPS)
                if which == 0:
                    act = act * GDN_DK ** -0.5
            dst[:, local + sub * LANES:local + (sub + 1) * LANES] = act.astype(BF16)

    n_conv = conv_dim // COL_TILE
    n_z = GDN_HEADS * GDN_DV // COL_TILE
    tasks = [(finish_conv, j, j * COL_TILE) for j in range(n_conv)]
    for j in range(n_z):
        tasks.insert((j + 1) * (n_conv + n_z) // (n_z + 1), (finish_z, j, conv_dim + j * COL_TILE))
    project = lambda c0: _dot(hn, w_ref[:, c0:c0 + COL_TILE])
    y = project(tasks[0][2])
    for i, (finish, j, _) in enumerate(tasks):
        y_next = project(tasks[i + 1][2]) if i + 1 < len(tasks) else None
        finish(j, y)
        y = y_next


def _gdn_in(x, norm_w, w_main, w_ba_t, conv_w, a_log, dt_bias):
    bsz, seq, d = x.shape
    ts = SEQ_TILE
    hv = GDN_HEADS * GDN_DV
    conv_dim = conv_w.shape[1]
    assert GDN_DK == LANES and GDN_DV == LANES and seq % ts == 0
    tok_spec = lambda width: pl.BlockSpec((None, ts, width), lambda b, s: (b, s, 0))
    gate_spec = pl.BlockSpec((None, GDN_HEADS, ts), lambda b, s: (b, 0, s))
    act = jax.ShapeDtypeStruct((bsz, seq, hv), BF16)
    gate = jax.ShapeDtypeStruct((bsz, GDN_HEADS, seq), F32)
    return pl.pallas_call(
        functools.partial(_gdn_in_kernel, ts=ts),
        grid=(bsz, seq // ts),
        in_specs=[tok_spec(d), _const_spec((1, d)), _const_spec(w_main.shape), _const_spec(w_ba_t.shape),
                  _const_spec(conv_w.shape), _const_spec((GDN_HEADS, 1)), _const_spec((GDN_HEADS, 1))],
        out_specs=[tok_spec(hv)] * 4 + [gate_spec] * 2,
        out_shape=[act] * 4 + [gate] * 2,
        scratch_shapes=[pltpu.VMEM((2, COL_TILE // LANES, SUBLANES + ts, LANES), F32),
                        pltpu.VMEM((SUBLANES, conv_dim), F32)],
        compiler_params=_params(),
        name="gdn_in",
    )(x, norm_w, w_main, w_ba_t, conv_w, a_log, dt_bias)


def _gdn_core_kernel(q_ref, k_ref, v_ref, z_ref, g_ref, beta_ref, nw_ref, o_ref,
                     gcum, u_s, wq_s, qkkd_s, gl_s, state, *, chunk):
    c = chunk
    wide = 2 * c
    nb, nh, n_chunks, _ = g_ref.shape
    row = lax.broadcasted_iota(jnp.int32, (c, wide), 0)
    col = lax.broadcasted_iota(jnp.int32, (c, wide), 1)
    lower = row >= col
    strict = row > col
    eye = row == col
    right = col >= c
    eye_right = jnp.where(col - c == row, 1.0, 0.0).astype(F32)
    eye_c = eye[:, :c]

    @pl.when(pl.program_id(1) == 0)
    def _():
        state[...] = jnp.zeros_like(state)

    upper_ones = jnp.where((row <= col) & (col < c), 1.0, 0.0).astype(BF16)
    g_all = g_ref[...].reshape(nb * nh * n_chunks, c)
    g_hi = g_all.astype(BF16)
    rem = g_all - g_hi.astype(F32)
    g_mid = rem.astype(BF16)
    g_lo = (rem - g_mid.astype(F32)).astype(BF16)
    gcum[...] = _dot(g_hi, upper_ones) + _dot(g_mid, upper_ones) + _dot(g_lo, upper_ones)

    def to_col(r, diag):
        return jnp.sum(jnp.where(diag, r, 0.0), axis=1, keepdims=True)

    seqs = [(b, h) for b in range(nb) for h in range(nh)]

    def chunk_rows(n, size=c, offset=0):
        if isinstance(n, int):
            return n * size + offset
        return pl.multiple_of(n * size + offset, c)

    wq_rows = 2 * c
    qkkd_rows = c + GDN_DK

    def prepare_products(n):
        rows = pl.ds(chunk_rows(n), c)
        x_list, rhs_list = [], []
        for b, h in seqs:
            seq_id = b * nh + h
            lanes = slice(h * GDN_DK, (h + 1) * GDN_DK)
            q = q_ref[b, rows, lanes]
            k = k_ref[b, rows, lanes]
            v = v_ref[b, rows, lanes]
            g_row = gcum[pl.ds(seq_id * n_chunks + n, 1), :]
            g_col = to_col(g_row, eye)
            b_col = to_col(beta_ref[b, h, pl.ds(n, 1), :], eye_c)
            g_last = g_row[:, c - 1:c]
            decay = jnp.where(lower, jnp.exp(g_col - g_row), 0.0)
            kf = k.astype(F32)
            kb = kf * b_col
            e_g = jnp.exp(g_col)
            lhs = jnp.concatenate([kb.astype(BF16), q], axis=0)
            prod = _dot_nt(lhs, jnp.concatenate([k, jnp.zeros_like(k)], axis=0))
            x_list.append(jnp.where(strict, prod[:c] * decay, 0.0) - eye_right)
            rhs_list.append(jnp.concatenate(
                [(v.astype(F32) * b_col).astype(BF16), (kb * e_g).astype(BF16)], axis=1))
            kd = kf * jnp.exp(g_last - g_col)
            qkkd_s[seq_id, pl.ds(chunk_rows(n, qkkd_rows), c), :] = (prod[c:] * decay)[:, :c].astype(BF16)
            qkkd_s[seq_id, pl.ds(chunk_rows(n, qkkd_rows, c), GDN_DK), :] = kd.T.astype(BF16)
            wq_s[seq_id, pl.ds(chunk_rows(n, wq_rows, c), c), :] = (q.astype(F32) * e_g).astype(BF16)
            gl_s[seq_id, pl.ds(n, 1), :] = jnp.broadcast_to(jnp.exp(g_last), (1, GDN_DV))
        return x_list, rhs_list

    def prepare_solve(n, x_list, rhs_list):
        levels = c.bit_length() - 1
        for level in range(levels):
            outs = []
            for x in x_list:
                xb = x.astype(BF16)
                outs.append(_dot(xb[:, :c], xb))
            if level == 0:
                x_list = [o + eye_right for o in outs]
            else:
                x_list = [o + jnp.where(right, x, 0.0) for o, x in zip(outs, x_list)]
        for (b, h), x, rhs in zip(seqs, x_list, rhs_list):
            seq_id = b * nh + h
            t_mat = pltpu.roll(x, c, axis=1)[:, :c].astype(BF16)
            uw = _dot(t_mat, rhs)
            u_s[seq_id, pl.ds(chunk_rows(n), c), :] = uw[:, :GDN_DV]
            wq_s[seq_id, pl.ds(chunk_rows(n, wq_rows), c), :] = uw[:, GDN_DV:].astype(BF16)

    nw = nw_ref[...]

    def step_read(n):
        st_list, vb_list, o_list = [], [], []
        for b, h in seqs:
            seq_id = b * nh + h
            st = state[seq_id]
            from_state = _dot(wq_s[seq_id, pl.ds(chunk_rows(n, wq_rows), wq_rows), :], st.astype(BF16))
            vb_list.append((u_s[seq_id, pl.ds(chunk_rows(n), c), :] - from_state[:c]).astype(BF16))
            o_list.append(from_state[c:])
            st_list.append(st)
        return st_list, vb_list, o_list

    def step_update(n, st_list, vb_list, o_list):
        rows = pl.ds(chunk_rows(n), c)
        for (b, h), st, vb, o in zip(seqs, st_list, vb_list, o_list):
            seq_id = b * nh + h
            lanes = slice(h * GDN_DV, (h + 1) * GDN_DV)
            from_v = _dot(qkkd_s[seq_id, pl.ds(chunk_rows(n, qkkd_rows), qkkd_rows), :], vb)
            o = o + from_v[:c]
            state[seq_id] = st * gl_s[seq_id, pl.ds(n, 1), :] + from_v[c:]
            o = o * lax.rsqrt(jnp.mean(o * o, axis=-1, keepdims=True) + NORM_EPS) * nw
            zz = z_ref[b, rows, lanes].astype(F32)
            o_ref[b, rows, lanes] = (o * (zz * _sigmoid(zz))).astype(BF16)

    prepare_solve(0, *prepare_products(0))

    def body(n, carry):
        partial_step = step_read(n)
        products = prepare_products(n + 1)
        step_update(n, *partial_step)
        prepare_solve(n + 1, *products)
        return carry

    lax.fori_loop(0, n_chunks - 1, body, 0)
    step_update(n_chunks - 1, *step_read(n_chunks - 1))


def _gdn_core(q, k, v, z, g, beta, norm_w):
    bsz, seq, width = q.shape
    c = GDN_CHUNK
    nb, sb = GDN_BATCH_BLOCK, GDN_SEQ_BLOCK
    assert bsz % nb == 0 and seq % sb == 0 and sb % c == 0
    n_chunks = sb // c
    n_seq = nb * GDN_HEADS
    g = g.reshape(bsz, GDN_HEADS, seq // c, c)
    beta = beta.reshape(bsz, GDN_HEADS, seq // c, c)
    tok_spec = pl.BlockSpec((nb, sb, width), lambda b, s: (b, s, 0))
    gate_spec = pl.BlockSpec((nb, GDN_HEADS, n_chunks, c), lambda b, s: (b, 0, s, 0))
    return pl.pallas_call(
        functools.partial(_gdn_core_kernel, chunk=c),
        grid=(bsz // nb, seq // sb),
        in_specs=[tok_spec] * 4 + [gate_spec] * 2 + [_const_spec((1, GDN_DV))],
        out_specs=tok_spec,
        out_shape=jax.ShapeDtypeStruct((bsz, seq, width), BF16),
        scratch_shapes=[pltpu.VMEM((n_seq * n_chunks, 2 * c), F32),
                        pltpu.VMEM((n_seq, sb, GDN_DV), F32),
                        pltpu.VMEM((n_seq, n_chunks * 2 * c, GDN_DK), BF16),
                        pltpu.VMEM((n_seq, n_chunks * (c + GDN_DK), c), BF16),
                        pltpu.VMEM((n_seq, n_chunks, GDN_DV), F32),
                        pltpu.VMEM((n_seq, GDN_DK, GDN_DV), F32)],
        compiler_params=_params(),
        name="gdn_core",
    )(q, k, v, z, g, beta, norm_w)


def _chan_mix_kernel(res_ref, mix_ref, wmix_ref, fnw_ref, win_ref, cw_ref, cb_ref, wout_ref,
                     pnw_ref, wgate_ref, p_ref, wproj_ref, onw_ref, out_ref, pbuf, carry, acc,
                     *, ts, d_ff, final_norm):
    h1 = res_ref[...] + _dot(mix_ref[...], wmix_ref[...])
    ple = _dot(p_ref[...].astype(BF16), wproj_ref[...])
    hn = _rms_norm(h1, fnw_ref[...]).astype(BF16)
    acc[...] = h1

    @pl.when(pl.program_id(1) == 0)
    def _():
        carry[...] = jnp.zeros_like(carry)

    ct = FFN_COL_TILE
    d_model = out_ref.shape[-1]

    def project(j):
        return tuple(_dot(hn, win_ref[:, c0:c0 + ct]) for c0 in (j * ct, d_ff + j * ct))

    def conv(y, c0, slot):
        outs = []
        for sub in range(ct // LANES):
            cols = slice(c0 + sub * LANES, c0 + (sub + 1) * LANES)
            stage = pbuf.at[slot, sub]
            stage[0:SUBLANES, :] = carry[:, cols]
            stage[SUBLANES:SUBLANES + ts, :] = y[:, sub * LANES:(sub + 1) * LANES]
            carry[:, cols] = y[ts - SUBLANES:, sub * LANES:(sub + 1) * LANES]
            out = cb_ref[:, cols]
            for tap in range(FFN_CONV):
                back = FFN_CONV - 1 - tap
                out = out + stage[SUBLANES - back:SUBLANES - back + ts, :] * cw_ref[tap:tap + 1, cols]
            outs.append(out)
        return jnp.concatenate(outs, axis=1)

    n_tiles = d_ff // ct
    ys = project(0)
    for j in range(n_tiles):
        gate = conv(ys[0], j * ct, 2 * (j % 2))
        up = conv(ys[1], d_ff + j * ct, 2 * (j % 2) + 1)
        hidden = (gate * _sigmoid(gate) * up).astype(BF16)
        if j + 1 < n_tiles:
            ys = project(j + 1)
        for n0 in range(0, d_model, OUT_COL_TILE):
            acc[:, n0:n0 + OUT_COL_TILE] += _dot(hidden, wout_ref[j * ct:(j + 1) * ct, n0:n0 + OUT_COL_TILE])

    h2 = acc[...]
    gate = _sigmoid(_dot(_rms_norm(h2, pnw_ref[...]).astype(BF16), wgate_ref[...]))
    h3 = h2 + gate * ple
    if final_norm:
        h3 = _rms_norm(h3, onw_ref[...])
    out_ref[...] = h3


def _chan_mix(res, mix, w_mix, ffn_norm, w_in, conv_w, conv_b, w_out, ple_norm, w_gate, p, w_proj,
              out_norm, final_norm):
    bsz, seq, d = res.shape
    ts = SEQ_TILE
    d_ff = w_out.shape[0]
    tok_spec = lambda width: pl.BlockSpec((None, ts, width), lambda b, s: (b, s, 0))
    consts = (w_mix, ffn_norm, w_in, conv_w, conv_b, w_out, ple_norm, w_gate)
    return pl.pallas_call(
        functools.partial(_chan_mix_kernel, ts=ts, d_ff=d_ff, final_norm=final_norm),
        grid=(bsz, seq // ts),
        in_specs=[tok_spec(d), tok_spec(mix.shape[-1])] + [_const_spec(a.shape) for a in consts]
                 + [tok_spec(p.shape[-1]), _const_spec(w_proj.shape), _const_spec(out_norm.shape)],
        out_specs=tok_spec(d),
        out_shape=jax.ShapeDtypeStruct((bsz, seq, d), F32),
        scratch_shapes=[pltpu.VMEM((4, FFN_COL_TILE // LANES, SUBLANES + ts, LANES), F32),
                        pltpu.VMEM((SUBLANES, 2 * d_ff), F32), pltpu.VMEM((ts, d), F32)],
        compiler_params=_params(),
        name="chan_mix",
    )(res, mix, *consts, p, w_proj, out_norm)


def _gelu(x):
    return 0.5 * x * (1.0 + lax.erf(x * (1.0 / math.sqrt(2.0))))


def _sgu_kernel(h_ref, nw_ref, win_ref, lng_ref, lnb_ref, ws_ref, bs_ref, out_ref, vbuf, *, ts, width):
    hn = _rms_norm(h_ref[...], nw_ref[...]).astype(BF16)
    n_col = width // SGU_COL_TILE
    for j in range(n_col):
        cols = slice(j * SGU_COL_TILE, (j + 1) * SGU_COL_TILE)
        vbuf[:, cols] = _gelu(_dot(hn, win_ref[:, width + j * SGU_COL_TILE:width + (j + 1) * SGU_COL_TILE]))
    v = vbuf[...]
    mu = jnp.mean(v, axis=-1, keepdims=True)
    vc = v - mu
    inv = lax.rsqrt(jnp.mean(vc * vc, axis=-1, keepdims=True) + LN_EPS)
    vbuf[...] = vc * inv * lng_ref[...] + lnb_ref[...]

    row = lax.broadcasted_iota(jnp.int32, (SGU_CHUNK, SGU_CHUNK), 0)
    col = lax.broadcasted_iota(jnp.int32, (SGU_CHUNK, SGU_CHUNK), 1)
    tri = row >= col
    gd = width // SGU_GROUPS
    for grp in range(SGU_GROUPS):
        cols = slice(grp * gd, (grp + 1) * gd)
        w_tri = jnp.where(tri, ws_ref[grp], 0.0).astype(BF16)
        bias = bs_ref[:, grp:grp + 1]
        u = _gelu(_dot(hn, win_ref[:, cols]))
        for i in range(ts // SGU_CHUNK):
            rows = slice(i * SGU_CHUNK, (i + 1) * SGU_CHUNK)
            mixed = _dot(w_tri, vbuf[rows, cols].astype(BF16)) + bias
            out_ref[rows, cols] = (u[rows, :] * mixed).astype(BF16)


def _sgu_mix(h, norm_w, w_in, ln_g, ln_b, w_s, b_s_t):
    bsz, seq, d = h.shape
    ts = SEQ_TILE
    width = w_in.shape[1] // 2
    tok_spec = lambda w: pl.BlockSpec((None, ts, w), lambda b, s: (b, s, 0))
    consts = (norm_w, w_in, ln_g, ln_b, w_s, b_s_t)
    return pl.pallas_call(
        functools.partial(_sgu_kernel, ts=ts, width=width),
        grid=(bsz, seq // ts),
        in_specs=[tok_spec(d)] + [_const_spec(a.shape) for a in consts],
        out_specs=tok_spec(width),
        out_shape=jax.ShapeDtypeStruct((bsz, seq, width), BF16),
        scratch_shapes=[pltpu.VMEM((ts, width), F32)],
        compiler_params=_params(),
        name="sgu_mix",
    )(h, *consts)


def kernel(x, p, mix_norm, gdn_w_in, gdn_conv_w, gdn_a_log, gdn_dt_bias, gdn_norm_w, gdn_w_out,
           sgu_w_in, sgu_ln_g, sgu_ln_b, sgu_w_s, sgu_b_s, sgu_w_out,
           ffn_norm, ffn_w_in, ffn_conv_w, ffn_conv_b, ffn_w_out,
           ple_norm, ple_w_gate, ple_w_proj, final_norm):
    row = lambda a: a.reshape(1, -1).astype(F32)
    bf = lambda a: a.astype(BF16)
    hv = GDN_HEADS * GDN_DV
    conv_dim = gdn_conv_w.shape[-1]

    def chan_mix(layer, res, mix, w_mix, final):
        return _chan_mix(res, mix, bf(w_mix), row(ffn_norm[layer]), bf(ffn_w_in[layer]), ffn_conv_w[layer],
                         row(ffn_conv_b[layer]), bf(ffn_w_out[layer]), row(ple_norm[layer]),
                         bf(ple_w_gate[layer]), p[layer], bf(ple_w_proj[layer]), row(final_norm), final)

    w_in = gdn_w_in[0]
    w_main = bf(w_in[:, :conv_dim + hv])
    w_ba_t = bf(w_in[:, conv_dim + hv:].T)
    q, k, v, z, g, beta = _gdn_in(x, row(mix_norm[0]), w_main, w_ba_t, gdn_conv_w[0],
                                  gdn_a_log[0].reshape(-1, 1), gdn_dt_bias[0].reshape(-1, 1))
    o = _gdn_core(q, k, v, z, g, beta, row(gdn_norm_w[0]))
    h = chan_mix(0, x, o, gdn_w_out[0], False)

    um = _sgu_mix(h, row(mix_norm[1]), bf(sgu_w_in[0]), row(sgu_ln_g[0]), row(sgu_ln_b[0]),
                  sgu_w_s[0], sgu_b_s[0].T)
    return chan_mix(1, h, um, sgu_w_out[0], True)
```

```python
import functools
import math

import jax
import jax.numpy as jnp
from jax import lax
from jax.experimental import pallas as pl
from jax.experimental.pallas import tpu as pltpu

F32 = jnp.float32
BF16 = jnp.bfloat16

NORM_EPS = 1e-6
LN_EPS = 1e-5
GDN_HEADS = 8
GDN_DK = 128
GDN_DV = 128
GDN_CONV = 4
SGU_CHUNK = 128
SGU_GROUPS = 8
FFN_CONV = 3

V7X_VMEM_BYTES = 64 * 1024 * 1024
VMEM_LIMIT_BYTES = V7X_VMEM_BYTES - 8 * 1024 * 1024
SUBLANES = 8
LANES = 128

SEQ_TILE = 512
GDN_CHUNK = 64
GDN_BATCH_BLOCK = 2
GDN_SEQ_BLOCK = 512
COL_TILE = 256
SGU_COL_TILE = 512
FFN_COL_TILE = 256
OUT_COL_TILE = 512


def _dot(a, b):
    return jnp.dot(a, b, preferred_element_type=F32)


def _dot_nt(a, b):
    return lax.dot_general(a, b, (((1,), (1,)), ((), ())), preferred_element_type=F32)


def _sigmoid(x):
    return 1.0 / (1.0 + jnp.exp(-x))


def _rms_norm(x, g):
    return x * lax.rsqrt(jnp.mean(x * x, axis=-1, keepdims=True) + NORM_EPS) * g


def _const_spec(shape):
    zeros = (0,) * len(shape)
    return pl.BlockSpec(shape, lambda *_: zeros, pipeline_mode=pl.Buffered(1))


def _params():
    return pltpu.CompilerParams(
        dimension_semantics=("arbitrary", "arbitrary"), vmem_limit_bytes=VMEM_LIMIT_BYTES)


def _gdn_in_kernel(x_ref, nw_ref, w_ref, wba_ref, cw_ref, alog_ref, dtb_ref,
                   q_ref, k_ref, v_ref, z_ref, g_ref, beta_ref, pbuf, carry, *, ts):
    qk_dim = GDN_HEADS * GDN_DK
    conv_dim = 2 * qk_dim + GDN_HEADS * GDN_DV
    hn = _rms_norm(x_ref[...], nw_ref[...]).astype(BF16)

    ba = _dot_nt(wba_ref[...], hn)
    beta_ref[...] = _sigmoid(ba[:GDN_HEADS])
    g_ref[...] = -jnp.exp(alog_ref[...]) * jax.nn.softplus(ba[GDN_HEADS:] + dtb_ref[...])

    @pl.when(pl.program_id(1) == 0)
    def _():
        carry[...] = jnp.zeros_like(carry)

    out_refs = (q_ref, k_ref, v_ref)

    def stage_z(j, y):
        z_ref[:, j * COL_TILE:(j + 1) * COL_TILE] = y.astype(BF16)

    def stage_conv(j, y):
        for sub in range(COL_TILE // LANES):
            cols = slice(j * COL_TILE + sub * LANES, j * COL_TILE + (sub + 1) * LANES)
            y_sub = y[:, sub * LANES:(sub + 1) * LANES]
            stage = pbuf.at[j % 2, sub]
            stage[0:SUBLANES, :] = carry[:, cols]
            stage[SUBLANES:SUBLANES + ts, :] = y_sub
            carry[:, cols] = y_sub[ts - SUBLANES:, :]

    def emit_conv(j):
        which, local = divmod(j * COL_TILE, qk_dim)
        dst = out_refs[which]
        for sub in range(COL_TILE // LANES):
            cols = slice(j * COL_TILE + sub * LANES, j * COL_TILE + (sub + 1) * LANES)
            stage = pbuf.at[j % 2, sub]
            acc = stage[SUBLANES - (GDN_CONV - 1):SUBLANES - (GDN_CONV - 1) + ts, :] * cw_ref[0:1, cols]
            for tap in range(1, GDN_CONV):
                back = GDN_CONV - 1 - tap
                acc = acc + stage[SUBLANES - back:SUBLANES - back + ts, :] * cw_ref[tap:tap + 1, cols]
            act = acc * _sigmoid(acc)
            if which < 2:
                act = act * lax.rsqrt(jnp.sum(act * act, axis=-1, keepdims=True) + NORM_EPS)
                if which == 0:
                    act = act * GDN_DK ** -0.5
            dst[:, local + sub * LANES:local + (sub + 1) * LANES] = act.astype(BF16)

    project = lambda c0: _dot(hn, w_ref[:, c0:c0 + COL_TILE])
    for j in range(GDN_HEADS * GDN_DV // COL_TILE):
        stage_z(j, project(conv_dim + j * COL_TILE))
    for j in range(conv_dim // COL_TILE):
        stage_conv(j, project(j * COL_TILE))
        emit_conv(j)


def _gdn_in(x, norm_w, w_main, w_ba_t, conv_w, a_log, dt_bias):
    bsz, seq, d = x.shape
    ts = SEQ_TILE
    hv = GDN_HEADS * GDN_DV
    conv_dim = conv_w.shape[1]
    assert GDN_DK == LANES and GDN_DV == LANES and seq % ts == 0
    tok_spec = lambda width: pl.BlockSpec((None, ts, width), lambda b, s: (b, s, 0))
    gate_spec = pl.BlockSpec((None, GDN_HEADS, ts), lambda b, s: (b, 0, s))
    act = jax.ShapeDtypeStruct((bsz, seq, hv), BF16)
    gate = jax.ShapeDtypeStruct((bsz, GDN_HEADS, seq), F32)
    return pl.pallas_call(
        functools.partial(_gdn_in_kernel, ts=ts),
        grid=(bsz, seq // ts),
        in_specs=[tok_spec(d), _const_spec((1, d)), _const_spec(w_main.shape), _const_spec(w_ba_t.shape),
                  _const_spec(conv_w.shape), _const_spec((GDN_HEADS, 1)), _const_spec((GDN_HEADS, 1))],
        out_specs=[tok_spec(hv)] * 4 + [gate_spec] * 2,
        out_shape=[act] * 4 + [gate] * 2,
        scratch_shapes=[pltpu.VMEM((2, COL_TILE // LANES, SUBLANES + ts, LANES), F32),
                        pltpu.VMEM((SUBLANES, conv_dim), F32)],
        compiler_params=_params(),
        name="gdn_in",
    )(x, norm_w, w_main, w_ba_t, conv_w, a_log, dt_bias)


def _gdn_core_kernel(q_ref, k_ref, v_ref, z_ref, g_ref, beta_ref, nw_ref, o_ref,
                     gcum, u_s, wq_s, qkkd_s, gl_s, state, *, chunk):
    c = chunk
    wide = 2 * c
    nb, nh, n_chunks, _ = g_ref.shape
    row = lax.broadcasted_iota(jnp.int32, (c, wide), 0)
    col = lax.broadcasted_iota(jnp.int32, (c, wide), 1)
    lower = row >= col
    strict = row > col
    eye = row == col
    right = col >= c
    eye_right = jnp.where(col - c == row, 1.0, 0.0).astype(F32)
    eye_c = eye[:, :c]

    @pl.when(pl.program_id(1) == 0)
    def _():
        state[...] = jnp.zeros_like(state)

    upper_ones = jnp.where((row <= col) & (col < c), 1.0, 0.0).astype(BF16)
    g_all = g_ref[...].reshape(nb * nh * n_chunks, c)
    g_hi = g_all.astype(BF16)
    rem = g_all - g_hi.astype(F32)
    g_mid = rem.astype(BF16)
    g_lo = (rem - g_mid.astype(F32)).astype(BF16)
    gcum[...] = _dot(g_hi, upper_ones) + _dot(g_mid, upper_ones) + _dot(g_lo, upper_ones)

    def to_col(r, diag):
        return jnp.sum(jnp.where(diag, r, 0.0), axis=1, keepdims=True)

    seqs = [(b, h) for b in range(nb) for h in range(nh)]

    def chunk_rows(n, size=c, offset=0):
        if isinstance(n, int):
            return n * size + offset
        return pl.multiple_of(n * size + offset, c)

    wq_rows = 2 * c
    qkkd_rows = c + GDN_DK

    def prepare_products(n):
        rows = pl.ds(chunk_rows(n), c)
        x_list, rhs_list = [], []
        for b, h in seqs:
            seq_id = b * nh + h
            lanes = slice(h * GDN_DK, (h + 1) * GDN_DK)
            q = q_ref[b, rows, lanes]
            k = k_ref[b, rows, lanes]
            v = v_ref[b, rows, lanes]
            g_row = gcum[pl.ds(seq_id * n_chunks + n, 1), :]
            g_col = to_col(g_row, eye)
            b_col = to_col(beta_ref[b, h, pl.ds(n, 1), :], eye_c)
            g_last = g_row[:, c - 1:c]
            decay = jnp.where(lower, jnp.exp(g_col - g_row), 0.0)
            kf = k.astype(F32)
            kb = kf * b_col
            e_g = jnp.exp(g_col)
            lhs = jnp.concatenate([kb.astype(BF16), q], axis=0)
            prod = _dot_nt(lhs, jnp.concatenate([k, jnp.zeros_like(k)], axis=0))
            x_list.append(jnp.where(strict, prod[:c] * decay, 0.0) - eye_right)
            rhs_list.append(jnp.concatenate(
                [(v.astype(F32) * b_col).astype(BF16), (kb * e_g).astype(BF16)], axis=1))
            kd = kf * jnp.exp(g_last - g_col)
            qkkd_s[seq_id, pl.ds(chunk_rows(n, qkkd_rows), c), :] = (prod[c:] * decay)[:, :c].astype(BF16)
            qkkd_s[seq_id, pl.ds(chunk_rows(n, qkkd_rows, c), GDN_DK), :] = kd.T.astype(BF16)
            wq_s[seq_id, pl.ds(chunk_rows(n, wq_rows, c), c), :] = (q.astype(F32) * e_g).astype(BF16)
            gl_s[seq_id, pl.ds(n, 1), :] = jnp.broadcast_to(jnp.exp(g_last), (1, GDN_DV))
        return x_list, rhs_list

    def prepare_solve(n, x_list, rhs_list):
        levels = c.bit_length() - 1
        for level in range(levels):
            outs = []
            for x in x_list:
                xb = x.astype(BF16)
                outs.append(_dot(xb[:, :c], xb))
            if level == 0:
                x_list = [o + eye_right for o in outs]
            else:
                x_list = [o + jnp.where(right, x, 0.0) for o, x in zip(outs, x_list)]
        for (b, h), x, rhs in zip(seqs, x_list, rhs_list):
            seq_id = b * nh + h
            t_mat = pltpu.roll(x, c, axis=1)[:, :c].astype(BF16)
            uw = _dot(t_mat, rhs)
            u_s[seq_id, pl.ds(chunk_rows(n), c), :] = uw[:, :GDN_DV]
            wq_s[seq_id, pl.ds(chunk_rows(n, wq_rows), c), :] = uw[:, GDN_DV:].astype(BF16)

    nw = nw_ref[...]

    def step_read(n):
        carried = []
        for b, h in seqs:
            seq_id = b * nh + h
            st = state[seq_id]
            from_state = _dot(wq_s[seq_id, pl.ds(chunk_rows(n, wq_rows), wq_rows), :], st.astype(BF16))
            vb = (u_s[seq_id, pl.ds(chunk_rows(n), c), :] - from_state[:c]).astype(BF16)
            lhs = qkkd_s[seq_id, pl.ds(chunk_rows(n, qkkd_rows), qkkd_rows), :]
            carried.append((gl_s[seq_id, pl.ds(n, 1), :], vb, from_state[c:], lhs))
        return carried

    def step_update(n, carried):
        rows = pl.ds(chunk_rows(n), c)
        for (b, h), (g_last, vb, o, lhs) in zip(seqs, carried):
            seq_id = b * nh + h
            lanes = slice(h * GDN_DV, (h + 1) * GDN_DV)
            from_v = _dot(lhs, vb)
            o = o + from_v[:c]
            state[seq_id] = state[seq_id] * g_last + from_v[c:]
            o = o * lax.rsqrt(jnp.mean(o * o, axis=-1, keepdims=True) + NORM_EPS) * nw
            zz = z_ref[b, rows, lanes].astype(F32)
            o_ref[b, rows, lanes] = (o * (zz * _sigmoid(zz))).astype(BF16)

    prepare_solve(0, *prepare_products(0))

    def body(n, carry):
        carried = step_read(n)
        products = prepare_products(n + 1)
        step_update(n, carried)
        prepare_solve(n + 1, *products)
        return carry

    lax.fori_loop(0, n_chunks - 1, body, 0)
    step_update(n_chunks - 1, step_read(n_chunks - 1))


def _gdn_core(q, k, v, z, g, beta, norm_w):
    bsz, seq, width = q.shape
    c = GDN_CHUNK
    nb, sb = GDN_BATCH_BLOCK, GDN_SEQ_BLOCK
    assert bsz % nb == 0 and seq % sb == 0 and sb % c == 0
    n_chunks = sb // c
    n_seq = nb * GDN_HEADS
    g = g.reshape(bsz, GDN_HEADS, seq // c, c)
    beta = beta.reshape(bsz, GDN_HEADS, seq // c, c)
    tok_spec = pl.BlockSpec((nb, sb, width), lambda b, s: (b, s, 0))
    gate_spec = pl.BlockSpec((nb, GDN_HEADS, n_chunks, c), lambda b, s: (b, 0, s, 0))
    return pl.pallas_call(
        functools.partial(_gdn_core_kernel, chunk=c),
        grid=(bsz // nb, seq // sb),
        in_specs=[tok_spec] * 4 + [gate_spec] * 2 + [_const_spec((1, GDN_DV))],
        out_specs=tok_spec,
        out_shape=jax.ShapeDtypeStruct((bsz, seq, width), BF16),
        scratch_shapes=[pltpu.VMEM((n_seq * n_chunks, 2 * c), F32),
                        pltpu.VMEM((n_seq, sb, GDN_DV), F32),
                        pltpu.VMEM((n_seq, n_chunks * 2 * c, GDN_DK), BF16),
                        pltpu.VMEM((n_seq, n_chunks * (c + GDN_DK), c), BF16),
                        pltpu.VMEM((n_seq, n_chunks, GDN_DV), F32),
                        pltpu.VMEM((n_seq, GDN_DK, GDN_DV), F32)],
        compiler_params=_params(),
        name="gdn_core",
    )(q, k, v, z, g, beta, norm_w)


def _chan_mix_kernel(res_ref, mix_ref, wmix_ref, fnw_ref, win_ref, cw_ref, cb_ref, wout_ref,
                     pnw_ref, wgate_ref, p_ref, wproj_ref, onw_ref, out_ref, pbuf, carry, acc,
                     *, ts, d_ff, final_norm):
    h1 = res_ref[...] + _dot(mix_ref[...], wmix_ref[...])
    ple = _dot(p_ref[...].astype(BF16), wproj_ref[...])
    hn = _rms_norm(h1, fnw_ref[...]).astype(BF16)
    acc[...] = h1

    @pl.when(pl.program_id(1) == 0)
    def _():
        carry[...] = jnp.zeros_like(carry)

    ct = FFN_COL_TILE
    d_model = out_ref.shape[-1]

    def project(j):
        return tuple(_dot(hn, win_ref[:, c0:c0 + ct]) for c0 in (j * ct, d_ff + j * ct))

    def conv(y, c0, slot):
        outs = []
        for sub in range(ct // LANES):
            cols = slice(c0 + sub * LANES, c0 + (sub + 1) * LANES)
            stage = pbuf.at[slot, sub]
            stage[0:SUBLANES, :] = carry[:, cols]
            stage[SUBLANES:SUBLANES + ts, :] = y[:, sub * LANES:(sub + 1) * LANES]
            carry[:, cols] = y[ts - SUBLANES:, sub * LANES:(sub + 1) * LANES]
            out = cb_ref[:, cols]
            for tap in range(FFN_CONV):
                back = FFN_CONV - 1 - tap
                out = out + stage[SUBLANES - back:SUBLANES - back + ts, :] * cw_ref[tap:tap + 1, cols]
            outs.append(out)
        return jnp.concatenate(outs, axis=1)

    def project_out(j, hidden):
        for n0 in range(0, d_model, OUT_COL_TILE):
            acc[:, n0:n0 + OUT_COL_TILE] += _dot(hidden, wout_ref[j * ct:(j + 1) * ct, n0:n0 + OUT_COL_TILE])

    n_tiles = d_ff // ct
    ys = project(0)
    hidden_prev = None
    for j in range(n_tiles):
        gate = conv(ys[0], j * ct, 2 * (j % 2))
        up = conv(ys[1], d_ff + j * ct, 2 * (j % 2) + 1)
        hidden = (gate * _sigmoid(gate) * up).astype(BF16)
        if j + 1 < n_tiles:
            ys = project(j + 1)
        if hidden_prev is not None:
            project_out(j - 1, hidden_prev)
        hidden_prev = hidden
    project_out(n_tiles - 1, hidden_prev)

    h2 = acc[...]
    gate = _sigmoid(_dot(_rms_norm(h2, pnw_ref[...]).astype(BF16), wgate_ref[...]))
    h3 = h2 + gate * ple
    if final_norm:
        h3 = _rms_norm(h3, onw_ref[...])
    out_ref[...] = h3


def _chan_mix(res, mix, w_mix, ffn_norm, w_in, conv_w, conv_b, w_out, ple_norm, w_gate, p, w_proj,
              out_norm, layer, final_norm):
    bsz, seq, d = res.shape
    ts = SEQ_TILE
    d_ff = w_out.shape[0]
    tok_spec = lambda width: pl.BlockSpec((None, ts, width), lambda b, s: (b, s, 0))
    consts = (w_mix, ffn_norm, w_in, conv_w, conv_b, w_out, ple_norm, w_gate)
    return pl.pallas_call(
        functools.partial(_chan_mix_kernel, ts=ts, d_ff=d_ff, final_norm=final_norm),
        grid=(bsz, seq // ts),
        in_specs=[tok_spec(d), tok_spec(mix.shape[-1])] + [_const_spec(a.shape) for a in consts]
                 + [pl.BlockSpec((None, None, ts, p.shape[-1]), lambda b, s: (layer, b, s, 0)),
                    _const_spec(w_proj.shape), _const_spec(out_norm.shape)],
        out_specs=tok_spec(d),
        out_shape=jax.ShapeDtypeStruct((bsz, seq, d), F32),
        scratch_shapes=[pltpu.VMEM((4, FFN_COL_TILE // LANES, SUBLANES + ts, LANES), F32),
                        pltpu.VMEM((SUBLANES, 2 * d_ff), F32), pltpu.VMEM((ts, d), F32)],
        compiler_params=_params(),
        name="chan_mix",
    )(res, mix, *consts, p, w_proj, out_norm)


def _gelu(x):
    return 0.5 * x * (1.0 + lax.erf(x * (1.0 / math.sqrt(2.0))))


def _sgu_kernel(h_ref, nw_ref, win_ref, lng_ref, lnb_ref, ws_ref, bs_ref, out_ref, vbuf, ubuf,
                *, ts, width):
    hn = _rms_norm(h_ref[...], nw_ref[...]).astype(BF16)
    for buf, base in ((vbuf, width), (ubuf, 0)):
        for c0 in range(0, width, SGU_COL_TILE):
            buf[:, c0:c0 + SGU_COL_TILE] = _gelu(_dot(hn, win_ref[:, base + c0:base + c0 + SGU_COL_TILE]))
    v = vbuf[...]
    mu = jnp.mean(v, axis=-1, keepdims=True)
    vc = v - mu
    inv = lax.rsqrt(jnp.mean(vc * vc, axis=-1, keepdims=True) + LN_EPS)
    vbuf[...] = vc * inv * lng_ref[...] + lnb_ref[...]

    row = lax.broadcasted_iota(jnp.int32, (SGU_CHUNK, SGU_CHUNK), 0)
    col = lax.broadcasted_iota(jnp.int32, (SGU_CHUNK, SGU_CHUNK), 1)
    tri = row >= col
    gd = width // SGU_GROUPS
    for grp in range(SGU_GROUPS):
        cols = slice(grp * gd, (grp + 1) * gd)
        w_tri = jnp.where(tri, ws_ref[grp], 0.0).astype(BF16)
        bias = bs_ref[:, grp:grp + 1]
        for i in range(ts // SGU_CHUNK):
            rows = slice(i * SGU_CHUNK, (i + 1) * SGU_CHUNK)
            mixed = _dot(w_tri, vbuf[rows, cols].astype(BF16)) + bias
            out_ref[rows, cols] = (ubuf[rows, cols] * mixed).astype(BF16)


def _sgu_mix(h, norm_w, w_in, ln_g, ln_b, w_s, b_s_t):
    bsz, seq, d = h.shape
    ts = SEQ_TILE
    width = w_in.shape[1] // 2
    tok_spec = lambda w: pl.BlockSpec((None, ts, w), lambda b, s: (b, s, 0))
    consts = (norm_w, w_in, ln_g, ln_b, w_s, b_s_t)
    return pl.pallas_call(
        functools.partial(_sgu_kernel, ts=ts, width=width),
        grid=(bsz, seq // ts),
        in_specs=[tok_spec(d)] + [_const_spec(a.shape) for a in consts],
        out_specs=tok_spec(width),
        out_shape=jax.ShapeDtypeStruct((bsz, seq, width), BF16),
        scratch_shapes=[pltpu.VMEM((ts, width), F32), pltpu.VMEM((ts, width), F32)],
        compiler_params=_params(),
        name="sgu_mix",
    )(h, *consts)


def kernel(x, p, mix_norm, gdn_w_in, gdn_conv_w, gdn_a_log, gdn_dt_bias, gdn_norm_w, gdn_w_out,
           sgu_w_in, sgu_ln_g, sgu_ln_b, sgu_w_s, sgu_b_s, sgu_w_out,
           ffn_norm, ffn_w_in, ffn_conv_w, ffn_conv_b, ffn_w_out,
           ple_norm, ple_w_gate, ple_w_proj, final_norm):
    row = lambda a: a.reshape(1, -1).astype(F32)
    bf = lambda a: a.astype(BF16)
    hv = GDN_HEADS * GDN_DV
    conv_dim = gdn_conv_w.shape[-1]

    def chan_mix(layer, res, mix, w_mix, final):
        return _chan_mix(res, mix, bf(w_mix), row(ffn_norm[layer]), bf(ffn_w_in[layer]), ffn_conv_w[layer],
                         row(ffn_conv_b[layer]), bf(ffn_w_out[layer]), row(ple_norm[layer]),
                         bf(ple_w_gate[layer]), p, bf(ple_w_proj[layer]), row(final_norm), layer, final)

    w_in = gdn_w_in[0]
    w_main = bf(w_in[:, :conv_dim + hv])
    w_ba_t = bf(w_in[:, conv_dim + hv:].T)
    q, k, v, z, g, beta = _gdn_in(x, row(mix_norm[0]), w_main, w_ba_t, gdn_conv_w[0],
                                  gdn_a_log[0].reshape(-1, 1), gdn_dt_bias[0].reshape(-1, 1))
    o = _gdn_core(q, k, v, z, g, beta, row(gdn_norm_w[0]))
    h = chan_mix(0, x, o, gdn_w_out[0], False)

    um = _sgu_mix(h, row(mix_norm[1]), bf(sgu_w_in[0]), row(sgu_ln_g[0]), row(sgu_ln_b[0]),
                  sgu_w_s[0], sgu_b_s[0].T)
    return chan_mix(1, h, um, sgu_w_out[0], True)
```

```python
import functools
import math

import jax
import jax.numpy as jnp
from jax import lax
from jax.experimental import pallas as pl
from jax.experimental.pallas import tpu as pltpu

F32 = jnp.float32
BF16 = jnp.bfloat16

NORM_EPS = 1e-6
LN_EPS = 1e-5
GDN_HEADS = 8
GDN_DK = 128
GDN_DV = 128
GDN_CONV = 4
SGU_CHUNK = 128
SGU_GROUPS = 8
FFN_CONV = 3

V7X_VMEM_BYTES = 64 * 1024 * 1024
VMEM_LIMIT_BYTES = V7X_VMEM_BYTES - 8 * 1024 * 1024
SUBLANES = 8
LANES = 128

SEQ_TILE = 512
GDN_IN_SEQ_TILE = 1024
GDN_CHUNK = 64
GDN_BATCH_BLOCK = 2
GDN_SEQ_BLOCK = 512
COL_TILE = 256
SGU_COL_TILE = 512
FFN_COL_TILE = 256
OUT_COL_TILE = 512


def _dot(a, b):
    return jnp.dot(a, b, preferred_element_type=F32)


def _dot_nt(a, b):
    return lax.dot_general(a, b, (((1,), (1,)), ((), ())), preferred_element_type=F32)


def _sigmoid(x):
    return 1.0 / (1.0 + jnp.exp(-x))


def _rms_norm(x, g):
    return x * lax.rsqrt(jnp.mean(x * x, axis=-1, keepdims=True) + NORM_EPS) * g


def _const_spec(shape):
    zeros = (0,) * len(shape)
    return pl.BlockSpec(shape, lambda *_: zeros, pipeline_mode=pl.Buffered(1))


def _params():
    return pltpu.CompilerParams(
        dimension_semantics=("arbitrary", "arbitrary"), vmem_limit_bytes=VMEM_LIMIT_BYTES)


def _gdn_in_kernel(x_ref, nw_ref, w_ref, wba_ref, cw_ref, alog_ref, dtb_ref,
                   q_ref, k_ref, v_ref, z_ref, g_ref, beta_ref, pbuf, carry, *, ts):
    qk_dim = GDN_HEADS * GDN_DK
    conv_dim = 2 * qk_dim + GDN_HEADS * GDN_DV
    hn = _rms_norm(x_ref[...], nw_ref[...]).astype(BF16)

    ba = _dot_nt(wba_ref[...], hn)
    beta_ref[...] = _sigmoid(ba[:GDN_HEADS])
    g_ref[...] = -jnp.exp(alog_ref[...]) * jax.nn.softplus(ba[GDN_HEADS:] + dtb_ref[...])

    @pl.when(pl.program_id(1) == 0)
    def _():
        carry[...] = jnp.zeros_like(carry)

    out_refs = (q_ref, k_ref, v_ref)

    def stage_z(j, y):
        z_ref[:, j * COL_TILE:(j + 1) * COL_TILE] = y.astype(BF16)

    def stage_conv(j, y):
        for sub in range(COL_TILE // LANES):
            cols = slice(j * COL_TILE + sub * LANES, j * COL_TILE + (sub + 1) * LANES)
            y_sub = y[:, sub * LANES:(sub + 1) * LANES]
            stage = pbuf.at[j % 2, sub]
            stage[0:SUBLANES, :] = carry[:, cols]
            stage[SUBLANES:SUBLANES + ts, :] = y_sub
            carry[:, cols] = y_sub[ts - SUBLANES:, :]

    def emit_conv(j):
        which, local = divmod(j * COL_TILE, qk_dim)
        dst = out_refs[which]
        for sub in range(COL_TILE // LANES):
            cols = slice(j * COL_TILE + sub * LANES, j * COL_TILE + (sub + 1) * LANES)
            stage = pbuf.at[j % 2, sub]
            first = SUBLANES - (GDN_CONV - 1)
            acc = stage[first:first + ts, :] * cw_ref[0:1, cols]
            for tap in range(1, GDN_CONV):
                acc = acc + stage[first + tap:first + tap + ts, :] * cw_ref[tap:tap + 1, cols]
            act = acc * _sigmoid(acc)
            if which < 2:
                act = act * lax.rsqrt(jnp.sum(act * act, axis=-1, keepdims=True) + NORM_EPS)
                if which == 0:
                    act = act * GDN_DK ** -0.5
            dst[:, local + sub * LANES:local + (sub + 1) * LANES] = act.astype(BF16)

    project = lambda c0: _dot(hn, w_ref[:, c0:c0 + COL_TILE])
    for j in range(GDN_HEADS * GDN_DV // COL_TILE):
        stage_z(j, project(conv_dim + j * COL_TILE))
    for j in range(conv_dim // COL_TILE):
        stage_conv(j, project(j * COL_TILE))
        emit_conv(j)


def _gdn_in(x, norm_w, w_main, w_ba_t, conv_w, a_log, dt_bias):
    bsz, seq, d = x.shape
    ts = GDN_IN_SEQ_TILE
    hv = GDN_HEADS * GDN_DV
    conv_dim = conv_w.shape[1]
    assert GDN_DK == LANES and GDN_DV == LANES and seq % ts == 0
    tok_spec = lambda width: pl.BlockSpec((None, ts, width), lambda b, s: (b, s, 0))
    gate_spec = pl.BlockSpec((None, GDN_HEADS, ts), lambda b, s: (b, 0, s))
    act = jax.ShapeDtypeStruct((bsz, seq, hv), BF16)
    gate = jax.ShapeDtypeStruct((bsz, GDN_HEADS, seq), F32)
    return pl.pallas_call(
        functools.partial(_gdn_in_kernel, ts=ts),
        grid=(bsz, seq // ts),
        in_specs=[tok_spec(d), _const_spec((1, d)), _const_spec(w_main.shape), _const_spec(w_ba_t.shape),
                  _const_spec(conv_w.shape), _const_spec((GDN_HEADS, 1)), _const_spec((GDN_HEADS, 1))],
        out_specs=[tok_spec(hv)] * 4 + [gate_spec] * 2,
        out_shape=[act] * 4 + [gate] * 2,
        scratch_shapes=[pltpu.VMEM((2, COL_TILE // LANES, SUBLANES + ts, LANES), F32),
                        pltpu.VMEM((SUBLANES, conv_dim), F32)],
        compiler_params=_params(),
        name="gdn_in",
    )(x, norm_w, w_main, w_ba_t, conv_w, a_log, dt_bias)


def _gdn_core_kernel(q_ref, k_ref, v_ref, z_ref, g_ref, beta_ref, nw_ref, o_ref,
                     gcum, u_s, wq_s, qkkd_s, gl_s, state, *, chunk):
    c = chunk
    wide = 2 * c
    nb, nh, n_chunks, _ = g_ref.shape
    row = lax.broadcasted_iota(jnp.int32, (c, wide), 0)
    col = lax.broadcasted_iota(jnp.int32, (c, wide), 1)
    lower = row >= col
    strict = row > col
    eye = row == col
    right = col >= c
    eye_right = jnp.where(col - c == row, 1.0, 0.0).astype(F32)
    eye_c = eye[:, :c]

    @pl.when(pl.program_id(1) == 0)
    def _():
        state[...] = jnp.zeros_like(state)

    upper_ones = jnp.where((row <= col) & (col < c), 1.0, 0.0).astype(BF16)
    g_all = g_ref[...].reshape(nb * nh * n_chunks, c)
    g_hi = g_all.astype(BF16)
    rem = g_all - g_hi.astype(F32)
    g_mid = rem.astype(BF16)
    g_lo = (rem - g_mid.astype(F32)).astype(BF16)
    gcum[...] = _dot(g_hi, upper_ones) + _dot(g_mid, upper_ones) + _dot(g_lo, upper_ones)

    def to_col(r, diag):
        return jnp.sum(jnp.where(diag, r, 0.0), axis=1, keepdims=True)

    seqs = [(b, h) for b in range(nb) for h in range(nh)]

    def chunk_rows(n, size=c, offset=0):
        if isinstance(n, int):
            return n * size + offset
        return pl.multiple_of(n * size + offset, c)

    wq_rows = 2 * c
    qkkd_rows = c + GDN_DK

    def prepare_products(n):
        rows = pl.ds(chunk_rows(n), c)
        x_list, rhs_list = [], []
        for b, h in seqs:
            seq_id = b * nh + h
            lanes = slice(h * GDN_DK, (h + 1) * GDN_DK)
            q = q_ref[b, rows, lanes]
            k = k_ref[b, rows, lanes]
            v = v_ref[b, rows, lanes]
            g_row = gcum[pl.ds(seq_id * n_chunks + n, 1), :]
            g_col = to_col(g_row, eye)
            b_col = to_col(beta_ref[b, h, pl.ds(n, 1), :], eye_c)
            g_last = g_row[:, c - 1:c]
            decay = jnp.where(lower, jnp.exp(g_col - g_row), 0.0)
            kf = k.astype(F32)
            kb = kf * b_col
            e_g = jnp.exp(g_col)
            lhs = jnp.concatenate([kb.astype(BF16), q], axis=0)
            prod = _dot_nt(lhs, jnp.concatenate([k, jnp.zeros_like(k)], axis=0))
            x_list.append(jnp.where(strict, prod[:c] * decay, 0.0) - eye_right)
            rhs_list.append(jnp.concatenate(
                [(v.astype(F32) * b_col).astype(BF16), (kb * e_g).astype(BF16)], axis=1))
            kd = kf * jnp.exp(g_last - g_col)
            qkkd_s[seq_id, pl.ds(chunk_rows(n, qkkd_rows), c), :] = (prod[c:] * decay)[:, :c].astype(BF16)
            qkkd_s[seq_id, pl.ds(chunk_rows(n, qkkd_rows, c), GDN_DK), :] = kd.T.astype(BF16)
            wq_s[seq_id, pl.ds(chunk_rows(n, wq_rows, c), c), :] = (q.astype(F32) * e_g).astype(BF16)
            gl_s[seq_id, pl.ds(n, 1), :] = jnp.broadcast_to(jnp.exp(g_last), (1, GDN_DV))
        return x_list, rhs_list

    def prepare_solve(n, x_list, rhs_list):
        levels = c.bit_length() - 1
        for level in range(levels):
            outs = []
            for x in x_list:
                xb = x.astype(BF16)
                outs.append(_dot(xb[:, :c], xb))
            if level == 0:
                x_list = [o + eye_right for o in outs]
            else:
                x_list = [o + jnp.where(right, x, 0.0) for o, x in zip(outs, x_list)]
        for (b, h), x, rhs in zip(seqs, x_list, rhs_list):
            seq_id = b * nh + h
            t_mat = pltpu.roll(x, c, axis=1)[:, :c].astype(BF16)
            uw = _dot(t_mat, rhs)
            u_s[seq_id, pl.ds(chunk_rows(n), c), :] = uw[:, :GDN_DV]
            wq_s[seq_id, pl.ds(chunk_rows(n, wq_rows), c), :] = uw[:, GDN_DV:].astype(BF16)

    nw = nw_ref[...]

    def step_read(n):
        carried = []
        for b, h in seqs:
            seq_id = b * nh + h
            st = state[seq_id]
            from_state = _dot(wq_s[seq_id, pl.ds(chunk_rows(n, wq_rows), wq_rows), :], st.astype(BF16))
            vb = (u_s[seq_id, pl.ds(chunk_rows(n), c), :] - from_state[:c]).astype(BF16)
            lhs = qkkd_s[seq_id, pl.ds(chunk_rows(n, qkkd_rows), qkkd_rows), :]
            carried.append((gl_s[seq_id, pl.ds(n, 1), :], vb, from_state[c:], lhs))
        return carried

    def step_update(n, carried):
        rows = pl.ds(chunk_rows(n), c)
        for (b, h), (g_last, vb, o, lhs) in zip(seqs, carried):
            seq_id = b * nh + h
            lanes = slice(h * GDN_DV, (h + 1) * GDN_DV)
            from_v = _dot(lhs, vb)
            o = o + from_v[:c]
            state[seq_id] = state[seq_id] * g_last + from_v[c:]
            o = o * lax.rsqrt(jnp.mean(o * o, axis=-1, keepdims=True) + NORM_EPS) * nw
            zz = z_ref[b, rows, lanes].astype(F32)
            o_ref[b, rows, lanes] = (o * (zz * _sigmoid(zz))).astype(BF16)

    prepare_solve(0, *prepare_products(0))

    def body(n, carry):
        carried = step_read(n)
        products = prepare_products(n + 1)
        step_update(n, carried)
        prepare_solve(n + 1, *products)
        return carry

    lax.fori_loop(0, n_chunks - 1, body, 0)
    step_update(n_chunks - 1, step_read(n_chunks - 1))


def _gdn_core(q, k, v, z, g, beta, norm_w):
    bsz, seq, width = q.shape
    c = GDN_CHUNK
    nb, sb = GDN_BATCH_BLOCK, GDN_SEQ_BLOCK
    assert bsz % nb == 0 and seq % sb == 0 and sb % c == 0
    n_chunks = sb // c
    n_seq = nb * GDN_HEADS
    g = g.reshape(bsz, GDN_HEADS, seq // c, c)
    beta = beta.reshape(bsz, GDN_HEADS, seq // c, c)
    tok_spec = pl.BlockSpec((nb, sb, width), lambda b, s: (b, s, 0))
    gate_spec = pl.BlockSpec((nb, GDN_HEADS, n_chunks, c), lambda b, s: (b, 0, s, 0))
    return pl.pallas_call(
        functools.partial(_gdn_core_kernel, chunk=c),
        grid=(bsz // nb, seq // sb),
        in_specs=[tok_spec] * 4 + [gate_spec] * 2 + [_const_spec((1, GDN_DV))],
        out_specs=tok_spec,
        out_shape=jax.ShapeDtypeStruct((bsz, seq, width), BF16),
        scratch_shapes=[pltpu.VMEM((n_seq * n_chunks, 2 * c), F32),
                        pltpu.VMEM((n_seq, sb, GDN_DV), F32),
                        pltpu.VMEM((n_seq, n_chunks * 2 * c, GDN_DK), BF16),
                        pltpu.VMEM((n_seq, n_chunks * (c + GDN_DK), c), BF16),
                        pltpu.VMEM((n_seq, n_chunks, GDN_DV), F32),
                        pltpu.VMEM((n_seq, GDN_DK, GDN_DV), F32)],
        compiler_params=_params(),
        name="gdn_core",
    )(q, k, v, z, g, beta, norm_w)


def _chan_mix_kernel(res_ref, mix_ref, wmix_ref, fnw_ref, win_ref, cw_ref, cb_ref, wout_ref,
                     pnw_ref, wgate_ref, p_ref, wproj_ref, onw_ref, out_ref, pbuf, carry, acc,
                     *, ts, d_ff, final_norm):
    h1 = res_ref[...] + _dot(mix_ref[...], wmix_ref[...])
    ple = _dot(p_ref[...].astype(BF16), wproj_ref[...])
    hn = _rms_norm(h1, fnw_ref[...]).astype(BF16)
    acc[...] = h1

    @pl.when(pl.program_id(1) == 0)
    def _():
        carry[...] = jnp.zeros_like(carry)

    ct = FFN_COL_TILE
    d_model = out_ref.shape[-1]

    def project(j):
        return tuple(_dot(hn, win_ref[:, c0:c0 + ct]) for c0 in (j * ct, d_ff + j * ct))

    def conv(y, c0, slot):
        outs = []
        for sub in range(ct // LANES):
            cols = slice(c0 + sub * LANES, c0 + (sub + 1) * LANES)
            stage = pbuf.at[slot, sub]
            stage[0:SUBLANES, :] = carry[:, cols]
            stage[SUBLANES:SUBLANES + ts, :] = y[:, sub * LANES:(sub + 1) * LANES]
            carry[:, cols] = y[ts - SUBLANES:, sub * LANES:(sub + 1) * LANES]
            out = cb_ref[:, cols]
            for tap in range(FFN_CONV):
                back = FFN_CONV - 1 - tap
                out = out + stage[SUBLANES - back:SUBLANES - back + ts, :] * cw_ref[tap:tap + 1, cols]
            outs.append(out)
        return jnp.concatenate(outs, axis=1)

    def project_out(j, hidden):
        for n0 in range(0, d_model, OUT_COL_TILE):
            acc[:, n0:n0 + OUT_COL_TILE] += _dot(hidden, wout_ref[j * ct:(j + 1) * ct, n0:n0 + OUT_COL_TILE])

    n_tiles = d_ff // ct
    ys = project(0)
    hidden_prev = None
    for j in range(n_tiles):
        gate = conv(ys[0], j * ct, 2 * (j % 2))
        up = conv(ys[1], d_ff + j * ct, 2 * (j % 2) + 1)
        hidden = (gate * _sigmoid(gate) * up).astype(BF16)
        if j + 1 < n_tiles:
            ys = project(j + 1)
        if hidden_prev is not None:
            project_out(j - 1, hidden_prev)
        hidden_prev = hidden
    project_out(n_tiles - 1, hidden_prev)

    h2 = acc[...]
    gate = _sigmoid(_dot(_rms_norm(h2, pnw_ref[...]).astype(BF16), wgate_ref[...]))
    h3 = h2 + gate * ple
    if final_norm:
        h3 = _rms_norm(h3, onw_ref[...])
    out_ref[...] = h3


def _chan_mix(res, mix, w_mix, ffn_norm, w_in, conv_w, conv_b, w_out, ple_norm, w_gate, p, w_proj,
              out_norm, layer, final_norm):
    bsz, seq, d = res.shape
    ts = SEQ_TILE
    d_ff = w_out.shape[0]
    tok_spec = lambda width: pl.BlockSpec((None, ts, width), lambda b, s: (b, s, 0))
    consts = (w_mix, ffn_norm, w_in, conv_w, conv_b, w_out, ple_norm, w_gate)
    return pl.pallas_call(
        functools.partial(_chan_mix_kernel, ts=ts, d_ff=d_ff, final_norm=final_norm),
        grid=(bsz, seq // ts),
        in_specs=[tok_spec(d), tok_spec(mix.shape[-1])] + [_const_spec(a.shape) for a in consts]
                 + [pl.BlockSpec((None, None, ts, p.shape[-1]), lambda b, s: (layer, b, s, 0)),
                    _const_spec(w_proj.shape), _const_spec(out_norm.shape)],
        out_specs=tok_spec(d),
        out_shape=jax.ShapeDtypeStruct((bsz, seq, d), F32),
        scratch_shapes=[pltpu.VMEM((4, FFN_COL_TILE // LANES, SUBLANES + ts, LANES), F32),
                        pltpu.VMEM((SUBLANES, 2 * d_ff), F32), pltpu.VMEM((ts, d), F32)],
        compiler_params=_params(),
        name="chan_mix",
    )(res, mix, *consts, p, w_proj, out_norm)


def _gelu(x):
    return 0.5 * x * (1.0 + lax.erf(x * (1.0 / math.sqrt(2.0))))


def _sgu_kernel(h_ref, nw_ref, win_ref, lng_ref, lnb_ref, ws_ref, bs_ref, out_ref, vbuf, ubuf,
                *, ts, width):
    hn = _rms_norm(h_ref[...], nw_ref[...]).astype(BF16)
    for buf, base in ((vbuf, width), (ubuf, 0)):
        for c0 in range(0, width, SGU_COL_TILE):
            buf[:, c0:c0 + SGU_COL_TILE] = _gelu(_dot(hn, win_ref[:, base + c0:base + c0 + SGU_COL_TILE]))
    v = vbuf[...]
    mu = jnp.mean(v, axis=-1, keepdims=True)
    vc = v - mu
    inv = lax.rsqrt(jnp.mean(vc * vc, axis=-1, keepdims=True) + LN_EPS)
    vbuf[...] = vc * inv * lng_ref[...] + lnb_ref[...]

    row = lax.broadcasted_iota(jnp.int32, (SGU_CHUNK, SGU_CHUNK), 0)
    col = lax.broadcasted_iota(jnp.int32, (SGU_CHUNK, SGU_CHUNK), 1)
    tri = row >= col
    gd = width // SGU_GROUPS
    for grp in range(SGU_GROUPS):
        cols = slice(grp * gd, (grp + 1) * gd)
        w_tri = jnp.where(tri, ws_ref[grp], 0.0).astype(BF16)
        bias = bs_ref[:, grp:grp + 1]
        for i in range(ts // SGU_CHUNK):
            rows = slice(i * SGU_CHUNK, (i + 1) * SGU_CHUNK)
            mixed = _dot(w_tri, vbuf[rows, cols].astype(BF16)) + bias
            out_ref[rows, cols] = (ubuf[rows, cols] * mixed).astype(BF16)


def _sgu_mix(h, norm_w, w_in, ln_g, ln_b, w_s, b_s_t):
    bsz, seq, d = h.shape
    ts = SEQ_TILE
    width = w_in.shape[1] // 2
    tok_spec = lambda w: pl.BlockSpec((None, ts, w), lambda b, s: (b, s, 0))
    consts = (norm_w, w_in, ln_g, ln_b, w_s, b_s_t)
    return pl.pallas_call(
        functools.partial(_sgu_kernel, ts=ts, width=width),
        grid=(bsz, seq // ts),
        in_specs=[tok_spec(d)] + [_const_spec(a.shape) for a in consts],
        out_specs=tok_spec(width),
        out_shape=jax.ShapeDtypeStruct((bsz, seq, width), BF16),
        scratch_shapes=[pltpu.VMEM((ts, width), F32), pltpu.VMEM((ts, width), F32)],
        compiler_params=_params(),
        name="sgu_mix",
    )(h, *consts)


def kernel(x, p, mix_norm, gdn_w_in, gdn_conv_w, gdn_a_log, gdn_dt_bias, gdn_norm_w, gdn_w_out,
           sgu_w_in, sgu_ln_g, sgu_ln_b, sgu_w_s, sgu_b_s, sgu_w_out,
           ffn_norm, ffn_w_in, ffn_conv_w, ffn_conv_b, ffn_w_out,
           ple_norm, ple_w_gate, ple_w_proj, final_norm):
    row = lambda a: a.reshape(1, -1).astype(F32)
    bf = lambda a: a.astype(BF16)
    hv = GDN_HEADS * GDN_DV
    conv_dim = gdn_conv_w.shape[-1]

    def chan_mix(layer, res, mix, w_mix, final):
        return _chan_mix(res, mix, bf(w_mix), row(ffn_norm[layer]), bf(ffn_w_in[layer]), ffn_conv_w[layer],
                         row(ffn_conv_b[layer]), bf(ffn_w_out[layer]), row(ple_norm[layer]),
                         bf(ple_w_gate[layer]), p, bf(ple_w_proj[layer]), row(final_norm), layer, final)

    w_in = gdn_w_in[0]
    w_main = bf(w_in[:, :conv_dim + hv])
    w_ba_t = bf(w_in[:, conv_dim + hv:].T)
    q, k, v, z, g, beta = _gdn_in(x, row(mix_norm[0]), w_main, w_ba_t, gdn_conv_w[0],
                                  gdn_a_log[0].reshape(-1, 1), gdn_dt_bias[0].reshape(-1, 1))
    o = _gdn_core(q, k, v, z, g, beta, row(gdn_norm_w[0]))
    h = chan_mix(0, x, o, gdn_w_out[0], False)

    um = _sgu_mix(h, row(mix_norm[1]), bf(sgu_w_in[0]), row(sgu_ln_g[0]), row(sgu_ln_b[0]),
                  sgu_w_s[0], sgu_b_s[0].T)
    return chan_mix(1, h, um, sgu_w_out[0], True)
```

```python
import functools
import math

import jax
import jax.numpy as jnp
from jax import lax
from jax.experimental import pallas as pl
from jax.experimental.pallas import tpu as pltpu

F32 = jnp.float32
BF16 = jnp.bfloat16

NORM_EPS = 1e-6
LN_EPS = 1e-5
GDN_HEADS = 8
GDN_DK = 128
GDN_DV = 128
GDN_CONV = 4
SGU_CHUNK = 128
SGU_GROUPS = 8
FFN_CONV = 3

V7X_VMEM_BYTES = 64 * 1024 * 1024
VMEM_LIMIT_BYTES = V7X_VMEM_BYTES - 8 * 1024 * 1024
SUBLANES = 8
LANES = 128

SEQ_TILE = 512
GDN_IN_SEQ_TILE = 1024
GDN_CHUNK = 64
GDN_BATCH_BLOCK = 2
GDN_SEQ_BLOCK = 512
COL_TILE = 256
SGU_COL_TILE = 512
FFN_COL_TILE = 256
OUT_PROJ_LAG = 1


def _dot(a, b):
    return jnp.dot(a, b, preferred_element_type=F32)


def _dot_nt(a, b):
    return lax.dot_general(a, b, (((1,), (1,)), ((), ())), preferred_element_type=F32)


def _sigmoid(x):
    return 1.0 / (1.0 + jnp.exp(-x))


def _rms_norm(x, g):
    return x * lax.rsqrt(jnp.mean(x * x, axis=-1, keepdims=True) + NORM_EPS) * g


def _const_spec(shape):
    zeros = (0,) * len(shape)
    return pl.BlockSpec(shape, lambda *_: zeros, pipeline_mode=pl.Buffered(1))


def _params():
    return pltpu.CompilerParams(
        dimension_semantics=("arbitrary", "arbitrary"), vmem_limit_bytes=VMEM_LIMIT_BYTES)


def _gdn_in_kernel(x_ref, nw_ref, w_ref, wba_ref, cw_ref, alog_ref, dtb_ref,
                   q_ref, k_ref, v_ref, z_ref, g_ref, beta_ref, pbuf, carry, *, ts):
    qk_dim = GDN_HEADS * GDN_DK
    conv_dim = 2 * qk_dim + GDN_HEADS * GDN_DV
    hn = _rms_norm(x_ref[...], nw_ref[...]).astype(BF16)

    ba = _dot_nt(wba_ref[...], hn)
    beta_ref[...] = _sigmoid(ba[:GDN_HEADS])
    g_ref[...] = -jnp.exp(alog_ref[...]) * jax.nn.softplus(ba[GDN_HEADS:] + dtb_ref[...])

    @pl.when(pl.program_id(1) == 0)
    def _():
        carry[...] = jnp.zeros_like(carry)

    out_refs = (q_ref, k_ref, v_ref)

    def stage_z(j, y):
        z_ref[:, j * COL_TILE:(j + 1) * COL_TILE] = y.astype(BF16)

    def stage_conv(j, y):
        for sub in range(COL_TILE // LANES):
            cols = slice(j * COL_TILE + sub * LANES, j * COL_TILE + (sub + 1) * LANES)
            y_sub = y[:, sub * LANES:(sub + 1) * LANES]
            stage = pbuf.at[j % 2, sub]
            stage[0:SUBLANES, :] = carry[:, cols]
            stage[SUBLANES:SUBLANES + ts, :] = y_sub
            carry[:, cols] = y_sub[ts - SUBLANES:, :]

    def emit_conv(j):
        which, local = divmod(j * COL_TILE, qk_dim)
        dst = out_refs[which]
        for sub in range(COL_TILE // LANES):
            cols = slice(j * COL_TILE + sub * LANES, j * COL_TILE + (sub + 1) * LANES)
            stage = pbuf.at[j % 2, sub]
            first = SUBLANES - (GDN_CONV - 1)
            acc = stage[first:first + ts, :] * cw_ref[0:1, cols]
            for tap in range(1, GDN_CONV):
                acc = acc + stage[first + tap:first + tap + ts, :] * cw_ref[tap:tap + 1, cols]
            act = acc * _sigmoid(acc)
            if which < 2:
                act = act * lax.rsqrt(jnp.sum(act * act, axis=-1, keepdims=True) + NORM_EPS)
                if which == 0:
                    act = act * GDN_DK ** -0.5
            dst[:, local + sub * LANES:local + (sub + 1) * LANES] = act.astype(BF16)

    project = lambda c0: _dot(hn, w_ref[:, c0:c0 + COL_TILE])
    for j in range(GDN_HEADS * GDN_DV // COL_TILE):
        stage_z(j, project(conv_dim + j * COL_TILE))
    for j in range(conv_dim // COL_TILE):
        stage_conv(j, project(j * COL_TILE))
        emit_conv(j)


def _gdn_in(x, norm_w, w_main, w_ba_t, conv_w, a_log, dt_bias):
    bsz, seq, d = x.shape
    ts = GDN_IN_SEQ_TILE
    hv = GDN_HEADS * GDN_DV
    conv_dim = conv_w.shape[1]
    assert GDN_DK == LANES and GDN_DV == LANES and seq % ts == 0
    tok_spec = lambda width: pl.BlockSpec((None, ts, width), lambda b, s: (b, s, 0))
    gate_spec = pl.BlockSpec((None, GDN_HEADS, ts), lambda b, s: (b, 0, s))
    act = jax.ShapeDtypeStruct((bsz, seq, hv), BF16)
    gate = jax.ShapeDtypeStruct((bsz, GDN_HEADS, seq), F32)
    return pl.pallas_call(
        functools.partial(_gdn_in_kernel, ts=ts),
        grid=(bsz, seq // ts),
        in_specs=[tok_spec(d), _const_spec((1, d)), _const_spec(w_main.shape), _const_spec(w_ba_t.shape),
                  _const_spec(conv_w.shape), _const_spec((GDN_HEADS, 1)), _const_spec((GDN_HEADS, 1))],
        out_specs=[tok_spec(hv)] * 4 + [gate_spec] * 2,
        out_shape=[act] * 4 + [gate] * 2,
        scratch_shapes=[pltpu.VMEM((2, COL_TILE // LANES, SUBLANES + ts, LANES), F32),
                        pltpu.VMEM((SUBLANES, conv_dim), F32)],
        compiler_params=_params(),
        name="gdn_in",
    )(x, norm_w, w_main, w_ba_t, conv_w, a_log, dt_bias)


def _gdn_core_kernel(q_ref, k_ref, v_ref, z_ref, g_ref, beta_ref, nw_ref, o_ref,
                     gcum, u_s, wq_s, qkkd_s, gl_s, state, *, chunk):
    c = chunk
    wide = 2 * c
    nb, nh, n_chunks, _ = g_ref.shape
    row = lax.broadcasted_iota(jnp.int32, (c, wide), 0)
    col = lax.broadcasted_iota(jnp.int32, (c, wide), 1)
    lower = row >= col
    strict = row > col
    eye = row == col
    right = col >= c
    eye_right = jnp.where(col - c == row, 1.0, 0.0).astype(F32)
    eye_c = eye[:, :c]

    @pl.when(pl.program_id(1) == 0)
    def _():
        state[...] = jnp.zeros_like(state)

    upper_ones = jnp.where((row <= col) & (col < c), 1.0, 0.0).astype(BF16)
    g_all = g_ref[...].reshape(nb * nh * n_chunks, c)
    g_hi = g_all.astype(BF16)
    rem = g_all - g_hi.astype(F32)
    g_mid = rem.astype(BF16)
    g_lo = (rem - g_mid.astype(F32)).astype(BF16)
    gcum[...] = _dot(g_hi, upper_ones) + _dot(g_mid, upper_ones) + _dot(g_lo, upper_ones)

    def to_col(r, diag):
        return jnp.sum(jnp.where(diag, r, 0.0), axis=1, keepdims=True)

    seqs = [(b, h) for b in range(nb) for h in range(nh)]

    def chunk_rows(n, size=c, offset=0):
        if isinstance(n, int):
            return n * size + offset
        return pl.multiple_of(n * size + offset, c)

    wq_rows = 2 * c
    qkkd_rows = c + GDN_DK

    def prepare_products(n):
        rows = pl.ds(chunk_rows(n), c)
        x_list, rhs_list = [], []
        for b, h in seqs:
            seq_id = b * nh + h
            lanes = slice(h * GDN_DK, (h + 1) * GDN_DK)
            q = q_ref[b, rows, lanes]
            k = k_ref[b, rows, lanes]
            v = v_ref[b, rows, lanes]
            g_row = gcum[pl.ds(seq_id * n_chunks + n, 1), :]
            g_col = to_col(g_row, eye)
            b_col = to_col(beta_ref[b, h, pl.ds(n, 1), :], eye_c)
            g_last = g_row[:, c - 1:c]
            decay = jnp.where(lower, jnp.exp(g_col - g_row), 0.0)
            kf = k.astype(F32)
            kb = kf * b_col
            e_g = jnp.exp(g_col)
            lhs = jnp.concatenate([kb.astype(BF16), q], axis=0)
            prod = _dot_nt(lhs, jnp.concatenate([k, jnp.zeros_like(k)], axis=0))
            x_list.append(jnp.where(strict, prod[:c] * decay, 0.0) - eye_right)
            rhs_list.append(jnp.concatenate(
                [(v.astype(F32) * b_col).astype(BF16), (kb * e_g).astype(BF16)], axis=1))
            kd = kf * jnp.exp(g_last - g_col)
            qkkd_s[seq_id, pl.ds(chunk_rows(n, qkkd_rows), c), :] = (prod[c:] * decay)[:, :c].astype(BF16)
            qkkd_s[seq_id, pl.ds(chunk_rows(n, qkkd_rows, c), GDN_DK), :] = kd.T.astype(BF16)
            wq_s[seq_id, pl.ds(chunk_rows(n, wq_rows, c), c), :] = (q.astype(F32) * e_g).astype(BF16)
            gl_s[seq_id, pl.ds(n, 1), :] = jnp.broadcast_to(jnp.exp(g_last), (1, GDN_DV))
        return x_list, rhs_list

    def prepare_solve(n, x_list, rhs_list):
        levels = c.bit_length() - 1
        for level in range(levels):
            outs = []
            for x in x_list:
                xb = x.astype(BF16)
                outs.append(_dot(xb[:, :c], xb))
            if level == 0:
                x_list = [o + eye_right for o in outs]
            else:
                x_list = [o + jnp.where(right, x, 0.0) for o, x in zip(outs, x_list)]
        for (b, h), x, rhs in zip(seqs, x_list, rhs_list):
            seq_id = b * nh + h
            t_mat = pltpu.roll(x, c, axis=1)[:, :c].astype(BF16)
            uw = _dot(t_mat, rhs)
            u_s[seq_id, pl.ds(chunk_rows(n), c), :] = uw[:, :GDN_DV]
            wq_s[seq_id, pl.ds(chunk_rows(n, wq_rows), c), :] = uw[:, GDN_DV:].astype(BF16)

    nw = nw_ref[...]

    def step_read(n):
        carried = []
        for b, h in seqs:
            seq_id = b * nh + h
            st = state[seq_id]
            from_state = _dot(wq_s[seq_id, pl.ds(chunk_rows(n, wq_rows), wq_rows), :], st.astype(BF16))
            vb = (u_s[seq_id, pl.ds(chunk_rows(n), c), :] - from_state[:c]).astype(BF16)
            lhs = qkkd_s[seq_id, pl.ds(chunk_rows(n, qkkd_rows), qkkd_rows), :]
            carried.append((gl_s[seq_id, pl.ds(n, 1), :], vb, from_state[c:], lhs))
        return carried

    def step_update(n, carried):
        rows = pl.ds(chunk_rows(n), c)
        for (b, h), (g_last, vb, o, lhs) in zip(seqs, carried):
            seq_id = b * nh + h
            lanes = slice(h * GDN_DV, (h + 1) * GDN_DV)
            from_v = _dot(lhs, vb)
            o = o + from_v[:c]
            state[seq_id] = state[seq_id] * g_last + from_v[c:]
            o = o * lax.rsqrt(jnp.mean(o * o, axis=-1, keepdims=True) + NORM_EPS) * nw
            zz = z_ref[b, rows, lanes].astype(F32)
            o_ref[b, rows, lanes] = (o * (zz * _sigmoid(zz))).astype(BF16)

    prepare_solve(0, *prepare_products(0))

    def body(n, carry):
        step_update(n, step_read(n))
        prepare_solve(n + 1, *prepare_products(n + 1))
        return carry

    lax.fori_loop(0, n_chunks - 1, body, 0)
    step_update(n_chunks - 1, step_read(n_chunks - 1))


def _gdn_core(q, k, v, z, g, beta, norm_w):
    bsz, seq, width = q.shape
    c = GDN_CHUNK
    nb, sb = GDN_BATCH_BLOCK, GDN_SEQ_BLOCK
    assert bsz % nb == 0 and seq % sb == 0 and sb % c == 0
    n_chunks = sb // c
    n_seq = nb * GDN_HEADS
    g = g.reshape(bsz, GDN_HEADS, seq // c, c)
    beta = beta.reshape(bsz, GDN_HEADS, seq // c, c)
    tok_spec = pl.BlockSpec((nb, sb, width), lambda b, s: (b, s, 0))
    gate_spec = pl.BlockSpec((nb, GDN_HEADS, n_chunks, c), lambda b, s: (b, 0, s, 0))
    return pl.pallas_call(
        functools.partial(_gdn_core_kernel, chunk=c),
        grid=(bsz // nb, seq // sb),
        in_specs=[tok_spec] * 4 + [gate_spec] * 2 + [_const_spec((1, GDN_DV))],
        out_specs=tok_spec,
        out_shape=jax.ShapeDtypeStruct((bsz, seq, width), BF16),
        scratch_shapes=[pltpu.VMEM((n_seq * n_chunks, 2 * c), F32),
                        pltpu.VMEM((n_seq, sb, GDN_DV), F32),
                        pltpu.VMEM((n_seq, n_chunks * 2 * c, GDN_DK), BF16),
                        pltpu.VMEM((n_seq, n_chunks * (c + GDN_DK), c), BF16),
                        pltpu.VMEM((n_seq, n_chunks, GDN_DV), F32),
                        pltpu.VMEM((n_seq, GDN_DK, GDN_DV), F32)],
        compiler_params=_params(),
        name="gdn_core",
    )(q, k, v, z, g, beta, norm_w)


def _chan_mix_kernel(res_ref, mix_ref, wmix_ref, fnw_ref, win_ref, cw_ref, cb_ref, wout_ref,
                     pnw_ref, wgate_ref, p_ref, wproj_ref, onw_ref, out_ref, pbuf, carry, acc,
                     *, ts, d_ff, final_norm):
    h1 = res_ref[...] + _dot(mix_ref[...], wmix_ref[...])
    ple = _dot(p_ref[...].astype(BF16), wproj_ref[...])
    hn = _rms_norm(h1, fnw_ref[...]).astype(BF16)
    acc[...] = h1

    @pl.when(pl.program_id(1) == 0)
    def _():
        carry[...] = jnp.zeros_like(carry)

    ct = FFN_COL_TILE

    def project(j):
        return tuple(_dot(hn, win_ref[:, c0:c0 + ct]) for c0 in (j * ct, d_ff + j * ct))

    def conv(y, c0, slot):
        outs = []
        for sub in range(ct // LANES):
            cols = slice(c0 + sub * LANES, c0 + (sub + 1) * LANES)
            stage = pbuf.at[slot, sub]
            stage[0:SUBLANES, :] = carry[:, cols]
            stage[SUBLANES:SUBLANES + ts, :] = y[:, sub * LANES:(sub + 1) * LANES]
            carry[:, cols] = y[ts - SUBLANES:, sub * LANES:(sub + 1) * LANES]
            out = cb_ref[:, cols]
            for tap in range(FFN_CONV):
                back = FFN_CONV - 1 - tap
                out = out + stage[SUBLANES - back:SUBLANES - back + ts, :] * cw_ref[tap:tap + 1, cols]
            outs.append(out)
        return jnp.concatenate(outs, axis=1)

    def project_out(j, hidden):
        acc[...] += _dot(hidden, wout_ref[j * ct:(j + 1) * ct, :])

    n_tiles = d_ff // ct
    ys = project(0)
    waiting = []
    for j in range(n_tiles):
        gate = conv(ys[0], j * ct, 2 * (j % 2))
        up = conv(ys[1], d_ff + j * ct, 2 * (j % 2) + 1)
        waiting.append((j, (gate * _sigmoid(gate) * up).astype(BF16)))
        if j + 1 < n_tiles:
            ys = project(j + 1)
        if len(waiting) > OUT_PROJ_LAG:
            project_out(*waiting.pop(0))
    for item in waiting:
        project_out(*item)

    h2 = acc[...]
    gate = _sigmoid(_dot(_rms_norm(h2, pnw_ref[...]).astype(BF16), wgate_ref[...]))
    h3 = h2 + gate * ple
    if final_norm:
        h3 = _rms_norm(h3, onw_ref[...])
    out_ref[...] = h3


def _chan_mix(res, mix, w_mix, ffn_norm, w_in, conv_w, conv_b, w_out, ple_norm, w_gate, p, w_proj,
              out_norm, layer, final_norm):
    bsz, seq, d = res.shape
    ts = SEQ_TILE
    d_ff = w_out.shape[0]
    tok_spec = lambda width: pl.BlockSpec((None, ts, width), lambda b, s: (b, s, 0))
    consts = (w_mix, ffn_norm, w_in, conv_w, conv_b, w_out, ple_norm, w_gate)
    return pl.pallas_call(
        functools.partial(_chan_mix_kernel, ts=ts, d_ff=d_ff, final_norm=final_norm),
        grid=(bsz, seq // ts),
        in_specs=[tok_spec(d), tok_spec(mix.shape[-1])] + [_const_spec(a.shape) for a in consts]
                 + [pl.BlockSpec((None, None, ts, p.shape[-1]), lambda b, s: (layer, b, s, 0)),
                    _const_spec(w_proj.shape), _const_spec(out_norm.shape)],
        out_specs=tok_spec(d),
        out_shape=jax.ShapeDtypeStruct((bsz, seq, d), F32),
        scratch_shapes=[pltpu.VMEM((4, FFN_COL_TILE // LANES, SUBLANES + ts, LANES), F32),
                        pltpu.VMEM((SUBLANES, 2 * d_ff), F32), pltpu.VMEM((ts, d), F32)],
        compiler_params=_params(),
        name="chan_mix",
    )(res, mix, *consts, p, w_proj, out_norm)


def _gelu(x):
    return 0.5 * x * (1.0 + lax.erf(x * (1.0 / math.sqrt(2.0))))


def _sgu_kernel(h_ref, nw_ref, win_ref, lng_ref, lnb_ref, ws_ref, bs_ref, out_ref, vbuf, ubuf,
                *, ts, width):
    hn = _rms_norm(h_ref[...], nw_ref[...]).astype(BF16)
    for buf, base in ((vbuf, width), (ubuf, 0)):
        for c0 in range(0, width, SGU_COL_TILE):
            buf[:, c0:c0 + SGU_COL_TILE] = _gelu(_dot(hn, win_ref[:, base + c0:base + c0 + SGU_COL_TILE]))
    v = vbuf[...]
    mu = jnp.mean(v, axis=-1, keepdims=True)
    vc = v - mu
    inv = lax.rsqrt(jnp.mean(vc * vc, axis=-1, keepdims=True) + LN_EPS)
    vbuf[...] = vc * inv * lng_ref[...] + lnb_ref[...]

    row = lax.broadcasted_iota(jnp.int32, (SGU_CHUNK, SGU_CHUNK), 0)
    col = lax.broadcasted_iota(jnp.int32, (SGU_CHUNK, SGU_CHUNK), 1)
    tri = row >= col
    gd = width // SGU_GROUPS
    for grp in range(SGU_GROUPS):
        cols = slice(grp * gd, (grp + 1) * gd)
        w_tri = jnp.where(tri, ws_ref[grp], 0.0).astype(BF16)
        bias = bs_ref[:, grp:grp + 1]
        for i in range(ts // SGU_CHUNK):
            rows = slice(i * SGU_CHUNK, (i + 1) * SGU_CHUNK)
            mixed = _dot(w_tri, vbuf[rows, cols].astype(BF16)) + bias
            out_ref[rows, cols] = (ubuf[rows, cols] * mixed).astype(BF16)


def _sgu_mix(h, norm_w, w_in, ln_g, ln_b, w_s, b_s_t):
    bsz, seq, d = h.shape
    ts = SEQ_TILE
    width = w_in.shape[1] // 2
    tok_spec = lambda w: pl.BlockSpec((None, ts, w), lambda b, s: (b, s, 0))
    consts = (norm_w, w_in, ln_g, ln_b, w_s, b_s_t)
    return pl.pallas_call(
        functools.partial(_sgu_kernel, ts=ts, width=width),
        grid=(bsz, seq // ts),
        in_specs=[tok_spec(d)] + [_const_spec(a.shape) for a in consts],
        out_specs=tok_spec(width),
        out_shape=jax.ShapeDtypeStruct((bsz, seq, width), BF16),
        scratch_shapes=[pltpu.VMEM((ts, width), F32), pltpu.VMEM((ts, width), F32)],
        compiler_params=_params(),
        name="sgu_mix",
    )(h, *consts)


def kernel(x, p, mix_norm, gdn_w_in, gdn_conv_w, gdn_a_log, gdn_dt_bias, gdn_norm_w, gdn_w_out,
           sgu_w_in, sgu_ln_g, sgu_ln_b, sgu_w_s, sgu_b_s, sgu_w_out,
           ffn_norm, ffn_w_in, ffn_conv_w, ffn_conv_b, ffn_w_out,
           ple_norm, ple_w_gate, ple_w_proj, final_norm):
    row = lambda a: a.reshape(1, -1).astype(F32)
    bf = lambda a: a.astype(BF16)
    hv = GDN_HEADS * GDN_DV
    conv_dim = gdn_conv_w.shape[-1]

    def chan_mix(layer, res, mix, w_mix, final):
        return _chan_mix(res, mix, bf(w_mix), row(ffn_norm[layer]), bf(ffn_w_in[layer]), ffn_conv_w[layer],
                         row(ffn_conv_b[layer]), bf(ffn_w_out[layer]), row(ple_norm[layer]),
                         bf(ple_w_gate[layer]), p, bf(ple_w_proj[layer]), row(final_norm), layer, final)

    w_in = gdn_w_in[0]
    w_main = bf(w_in[:, :conv_dim + hv])
    w_ba_t = bf(w_in[:, conv_dim + hv:].T)
    q, k, v, z, g, beta = _gdn_in(x, row(mix_norm[0]), w_main, w_ba_t, gdn_conv_w[0],
                                  gdn_a_log[0].reshape(-1, 1), gdn_dt_bias[0].reshape(-1, 1))
    o = _gdn_core(q, k, v, z, g, beta, row(gdn_norm_w[0]))
    h = chan_mix(0, x, o, gdn_w_out[0], False)

    um = _sgu_mix(h, row(mix_norm[1]), bf(sgu_w_in[0]), row(sgu_ln_g[0]), row(sgu_ln_b[0]),
                  sgu_w_s[0], sgu_b_s[0].T)
    return chan_mix(1, h, um, sgu_w_out[0], True)
```

```python
import functools
import math

import jax
import jax.numpy as jnp
from jax import lax
from jax.experimental import pallas as pl
from jax.experimental.pallas import tpu as pltpu

F32 = jnp.float32
BF16 = jnp.bfloat16

NORM_EPS = 1e-6
LN_EPS = 1e-5
GDN_HEADS = 8
GDN_DK = 128
GDN_DV = 128
GDN_CONV = 4
SGU_CHUNK = 128
SGU_GROUPS = 8
FFN_CONV = 3

V7X_VMEM_BYTES = 64 * 1024 * 1024
VMEM_LIMIT_BYTES = V7X_VMEM_BYTES - 8 * 1024 * 1024
SUBLANES = 8
LANES = 128

SEQ_TILE = 512
GDN_IN_SEQ_TILE = 1024
GDN_CHUNK = 64
SOLVE_BLOCK = 8
GDN_BATCH_BLOCK = 2
GDN_SEQ_BLOCK = 512
COL_TILE = 256
SGU_COL_TILE = 512
FFN_COL_TILE = 256
OUT_PROJ_LAG = 1


def _dot(a, b):
    return jnp.dot(a, b, preferred_element_type=F32)


def _dot_nt(a, b):
    return lax.dot_general(a, b, (((1,), (1,)), ((), ())), preferred_element_type=F32)


def _sigmoid(x):
    return 1.0 / (1.0 + jnp.exp(-x))


def _rms_norm(x, g):
    return x * lax.rsqrt(jnp.mean(x * x, axis=-1, keepdims=True) + NORM_EPS) * g


def _const_spec(shape):
    zeros = (0,) * len(shape)
    return pl.BlockSpec(shape, lambda *_: zeros, pipeline_mode=pl.Buffered(1))


def _params():
    return pltpu.CompilerParams(
        dimension_semantics=("arbitrary", "arbitrary"), vmem_limit_bytes=VMEM_LIMIT_BYTES)


def _gdn_in_kernel(x_ref, nw_ref, w_ref, wba_ref, cw_ref, alog_ref, dtb_ref,
                   q_ref, k_ref, v_ref, z_ref, g_ref, beta_ref, pbuf, carry, *, ts):
    qk_dim = GDN_HEADS * GDN_DK
    conv_dim = 2 * qk_dim + GDN_HEADS * GDN_DV
    hn = _rms_norm(x_ref[...], nw_ref[...]).astype(BF16)

    ba = _dot_nt(wba_ref[...], hn)
    beta_ref[...] = _sigmoid(ba[:GDN_HEADS])
    g_ref[...] = -jnp.exp(alog_ref[...]) * jax.nn.softplus(ba[GDN_HEADS:] + dtb_ref[...])

    @pl.when(pl.program_id(1) == 0)
    def _():
        carry[...] = jnp.zeros_like(carry)

    out_refs = (q_ref, k_ref, v_ref)

    def stage_z(j, y):
        z_ref[:, j * COL_TILE:(j + 1) * COL_TILE] = y.astype(BF16)

    def stage_conv(j, y):
        for sub in range(COL_TILE // LANES):
            cols = slice(j * COL_TILE + sub * LANES, j * COL_TILE + (sub + 1) * LANES)
            y_sub = y[:, sub * LANES:(sub + 1) * LANES]
            stage = pbuf.at[j % 2, sub]
            stage[0:SUBLANES, :] = carry[:, cols]
            stage[SUBLANES:SUBLANES + ts, :] = y_sub
            carry[:, cols] = y_sub[ts - SUBLANES:, :]

    def emit_conv(j):
        which, local = divmod(j * COL_TILE, qk_dim)
        dst = out_refs[which]
        for sub in range(COL_TILE // LANES):
            cols = slice(j * COL_TILE + sub * LANES, j * COL_TILE + (sub + 1) * LANES)
            stage = pbuf.at[j % 2, sub]
            first = SUBLANES - (GDN_CONV - 1)
            acc = stage[first:first + ts, :] * cw_ref[0:1, cols]
            for tap in range(1, GDN_CONV):
                acc = acc + stage[first + tap:first + tap + ts, :] * cw_ref[tap:tap + 1, cols]
            act = acc * _sigmoid(acc)
            if which < 2:
                act = act * lax.rsqrt(jnp.sum(act * act, axis=-1, keepdims=True) + NORM_EPS)
                if which == 0:
                    act = act * GDN_DK ** -0.5
            dst[:, local + sub * LANES:local + (sub + 1) * LANES] = act.astype(BF16)

    project = lambda c0: _dot(hn, w_ref[:, c0:c0 + COL_TILE])
    for j in range(GDN_HEADS * GDN_DV // COL_TILE):
        stage_z(j, project(conv_dim + j * COL_TILE))
    for j in range(conv_dim // COL_TILE):
        stage_conv(j, project(j * COL_TILE))
        emit_conv(j)


def _gdn_in(x, norm_w, w_main, w_ba_t, conv_w, a_log, dt_bias):
    bsz, seq, d = x.shape
    ts = GDN_IN_SEQ_TILE
    hv = GDN_HEADS * GDN_DV
    conv_dim = conv_w.shape[1]
    assert GDN_DK == LANES and GDN_DV == LANES and seq % ts == 0
    tok_spec = lambda width: pl.BlockSpec((None, ts, width), lambda b, s: (b, s, 0))
    gate_spec = pl.BlockSpec((None, GDN_HEADS, ts), lambda b, s: (b, 0, s))
    act = jax.ShapeDtypeStruct((bsz, seq, hv), BF16)
    gate = jax.ShapeDtypeStruct((bsz, GDN_HEADS, seq), F32)
    return pl.pallas_call(
        functools.partial(_gdn_in_kernel, ts=ts),
        grid=(bsz, seq // ts),
        in_specs=[tok_spec(d), _const_spec((1, d)), _const_spec(w_main.shape), _const_spec(w_ba_t.shape),
                  _const_spec(conv_w.shape), _const_spec((GDN_HEADS, 1)), _const_spec((GDN_HEADS, 1))],
        out_specs=[tok_spec(hv)] * 4 + [gate_spec] * 2,
        out_shape=[act] * 4 + [gate] * 2,
        scratch_shapes=[pltpu.VMEM((2, COL_TILE // LANES, SUBLANES + ts, LANES), F32),
                        pltpu.VMEM((SUBLANES, conv_dim), F32)],
        compiler_params=_params(),
        name="gdn_in",
    )(x, norm_w, w_main, w_ba_t, conv_w, a_log, dt_bias)


def _gdn_core_kernel(q_ref, k_ref, v_ref, z_ref, g_ref, beta_ref, nw_ref, o_ref,
                     gcum, u_s, wq_s, qkkd_s, gl_s, state, *, chunk):
    c = chunk
    wide = 2 * c
    nb, nh, n_chunks, _ = g_ref.shape
    row = lax.broadcasted_iota(jnp.int32, (c, wide), 0)
    col = lax.broadcasted_iota(jnp.int32, (c, wide), 1)
    lower = row >= col
    strict = row > col
    eye = row == col
    right = col >= c
    eye_right = jnp.where(col - c == row, 1.0, 0.0).astype(F32)
    eye_c = eye[:, :c]

    @pl.when(pl.program_id(1) == 0)
    def _():
        state[...] = jnp.zeros_like(state)

    upper_ones = jnp.where((row <= col) & (col < c), 1.0, 0.0).astype(BF16)
    g_all = g_ref[...].reshape(nb * nh * n_chunks, c)
    g_hi = g_all.astype(BF16)
    rem = g_all - g_hi.astype(F32)
    g_mid = rem.astype(BF16)
    g_lo = (rem - g_mid.astype(F32)).astype(BF16)
    gcum[...] = _dot(g_hi, upper_ones) + _dot(g_mid, upper_ones) + _dot(g_lo, upper_ones)

    def to_col(r, diag):
        return jnp.sum(jnp.where(diag, r, 0.0), axis=1, keepdims=True)

    seqs = [(b, h) for b in range(nb) for h in range(nh)]

    def chunk_rows(n, size=c, offset=0):
        if isinstance(n, int):
            return n * size + offset
        return pl.multiple_of(n * size + offset, c)

    wq_rows = 2 * c
    qkkd_rows = c + GDN_DK

    def prepare_products(n):
        rows = pl.ds(chunk_rows(n), c)
        x_list, rhs_list = [], []
        for b, h in seqs:
            seq_id = b * nh + h
            lanes = slice(h * GDN_DK, (h + 1) * GDN_DK)
            q = q_ref[b, rows, lanes]
            k = k_ref[b, rows, lanes]
            v = v_ref[b, rows, lanes]
            g_row = gcum[pl.ds(seq_id * n_chunks + n, 1), :]
            g_col = to_col(g_row, eye)
            b_col = to_col(beta_ref[b, h, pl.ds(n, 1), :], eye_c)
            g_last = g_row[:, c - 1:c]
            decay = jnp.where(lower, jnp.exp(g_col - g_row), 0.0)
            kf = k.astype(F32)
            kb = kf * b_col
            e_g = jnp.exp(g_col)
            lhs = jnp.concatenate([kb.astype(BF16), q], axis=0)
            prod = _dot_nt(lhs, jnp.concatenate([k, jnp.zeros_like(k)], axis=0))
            x_list.append(jnp.where(strict, prod[:c] * decay, 0.0))
            rhs_list.append(jnp.concatenate(
                [(v.astype(F32) * b_col).astype(BF16), (kb * e_g).astype(BF16)], axis=1))
            kd = kf * jnp.exp(g_last - g_col)
            qkkd_s[seq_id, pl.ds(chunk_rows(n, qkkd_rows), c), :] = (prod[c:] * decay)[:, :c].astype(BF16)
            qkkd_s[seq_id, pl.ds(chunk_rows(n, qkkd_rows, c), GDN_DK), :] = kd.T.astype(BF16)
            wq_s[seq_id, pl.ds(chunk_rows(n, wq_rows, c), c), :] = (q.astype(F32) * e_g).astype(BF16)
            gl_s[seq_id, pl.ds(n, 1), :] = jnp.broadcast_to(jnp.exp(g_last), (1, GDN_DV))
        return x_list, rhs_list

    def same_block(size):
        return row // size == col // size

    def prepare_solve(n, a_list, rhs_list):
        x_list = [jnp.where(same_block(SOLVE_BLOCK), a, 0.0) - eye_right for a in a_list]
        for level in range(SOLVE_BLOCK.bit_length() - 1):
            outs = []
            for x in x_list:
                xb = x.astype(BF16)
                outs.append(_dot(xb[:, :c], xb))
            if level == 0:
                x_list = [o + eye_right for o in outs]
            else:
                x_list = [o + jnp.where(right, x, 0.0) for o, x in zip(outs, x_list)]
        t_list = [pltpu.roll(x, c, axis=1) for x in x_list]
        size = 2 * SOLVE_BLOCK
        while size <= c:
            joins = [jnp.where(same_block(size) & ~same_block(size // 2), a, 0.0).astype(BF16) for a in a_list]
            tb_list = [t.astype(BF16) for t in t_list]
            lt_list = [_dot(lj[:, :c], tb).astype(BF16) for lj, tb in zip(joins, tb_list)]
            t_list = [t - _dot(tb[:, :c], lt) for t, tb, lt in zip(t_list, tb_list, lt_list)]
            size *= 2
        for (b, h), t, rhs in zip(seqs, t_list, rhs_list):
            seq_id = b * nh + h
            uw = _dot(t[:, :c].astype(BF16), rhs)
            u_s[seq_id, pl.ds(chunk_rows(n), c), :] = uw[:, :GDN_DV]
            wq_s[seq_id, pl.ds(chunk_rows(n, wq_rows), c), :] = uw[:, GDN_DV:].astype(BF16)

    nw = nw_ref[...]

    def step_read(n):
        carried = []
        for b, h in seqs:
            seq_id = b * nh + h
            st = state[seq_id]
            from_state = _dot(wq_s[seq_id, pl.ds(chunk_rows(n, wq_rows), wq_rows), :], st.astype(BF16))
            vb = (u_s[seq_id, pl.ds(chunk_rows(n), c), :] - from_state[:c]).astype(BF16)
            lhs = qkkd_s[seq_id, pl.ds(chunk_rows(n, qkkd_rows), qkkd_rows), :]
            carried.append((gl_s[seq_id, pl.ds(n, 1), :], vb, from_state[c:], lhs))
        return carried

    def step_update(n, carried):
        rows = pl.ds(chunk_rows(n), c)
        for (b, h), (g_last, vb, o, lhs) in zip(seqs, carried):
            seq_id = b * nh + h
            lanes = slice(h * GDN_DV, (h + 1) * GDN_DV)
            from_v = _dot(lhs, vb)
            o = o + from_v[:c]
            state[seq_id] = state[seq_id] * g_last + from_v[c:]
            o = o * lax.rsqrt(jnp.mean(o * o, axis=-1, keepdims=True) + NORM_EPS) * nw
            zz = z_ref[b, rows, lanes].astype(F32)
            o_ref[b, rows, lanes] = (o * (zz * _sigmoid(zz))).astype(BF16)

    prepare_solve(0, *prepare_products(0))

    def body(n, carry):
        carried = step_read(n)
        products = prepare_products(n + 1)
        step_update(n, carried)
        prepare_solve(n + 1, *products)
        return carry

    lax.fori_loop(0, n_chunks - 1, body, 0)
    step_update(n_chunks - 1, step_read(n_chunks - 1))


def _gdn_core(q, k, v, z, g, beta, norm_w):
    bsz, seq, width = q.shape
    c = GDN_CHUNK
    nb, sb = GDN_BATCH_BLOCK, GDN_SEQ_BLOCK
    assert bsz % nb == 0 and seq % sb == 0 and sb % c == 0
    n_chunks = sb // c
    n_seq = nb * GDN_HEADS
    g = g.reshape(bsz, GDN_HEADS, seq // c, c)
    beta = beta.reshape(bsz, GDN_HEADS, seq // c, c)
    tok_spec = pl.BlockSpec((nb, sb, width), lambda b, s: (b, s, 0))
    gate_spec = pl.BlockSpec((nb, GDN_HEADS, n_chunks, c), lambda b, s: (b, 0, s, 0))
    return pl.pallas_call(
        functools.partial(_gdn_core_kernel, chunk=c),
        grid=(bsz // nb, seq // sb),
        in_specs=[tok_spec] * 4 + [gate_spec] * 2 + [_const_spec((1, GDN_DV))],
        out_specs=tok_spec,
        out_shape=jax.ShapeDtypeStruct((bsz, seq, width), BF16),
        scratch_shapes=[pltpu.VMEM((n_seq * n_chunks, 2 * c), F32),
                        pltpu.VMEM((n_seq, sb, GDN_DV), F32),
                        pltpu.VMEM((n_seq, n_chunks * 2 * c, GDN_DK), BF16),
                        pltpu.VMEM((n_seq, n_chunks * (c + GDN_DK), c), BF16),
                        pltpu.VMEM((n_seq, n_chunks, GDN_DV), F32),
                        pltpu.VMEM((n_seq, GDN_DK, GDN_DV), F32)],
        compiler_params=_params(),
        name="gdn_core",
    )(q, k, v, z, g, beta, norm_w)


def _chan_mix_kernel(res_ref, mix_ref, wmix_ref, fnw_ref, win_ref, cw_ref, cb_ref, wout_ref,
                     pnw_ref, wgate_ref, p_ref, wproj_ref, onw_ref, out_ref, pbuf, carry, acc,
                     *, ts, d_ff, final_norm):
    h1 = res_ref[...] + _dot(mix_ref[...], wmix_ref[...])
    ple = _dot(p_ref[...].astype(BF16), wproj_ref[...])
    hn = _rms_norm(h1, fnw_ref[...]).astype(BF16)
    acc[...] = h1

    @pl.when(pl.program_id(1) == 0)
    def _():
        carry[...] = jnp.zeros_like(carry)

    ct = FFN_COL_TILE

    def project(j):
        return tuple(_dot(hn, win_ref[:, c0:c0 + ct]) for c0 in (j * ct, d_ff + j * ct))

    def conv(y, c0, slot):
        outs = []
        for sub in range(ct // LANES):
            cols = slice(c0 + sub * LANES, c0 + (sub + 1) * LANES)
            stage = pbuf.at[slot, sub]
            stage[0:SUBLANES, :] = carry[:, cols]
            stage[SUBLANES:SUBLANES + ts, :] = y[:, sub * LANES:(sub + 1) * LANES]
            carry[:, cols] = y[ts - SUBLANES:, sub * LANES:(sub + 1) * LANES]
            out = cb_ref[:, cols]
            for tap in range(FFN_CONV):
                back = FFN_CONV - 1 - tap
                out = out + stage[SUBLANES - back:SUBLANES - back + ts, :] * cw_ref[tap:tap + 1, cols]
            outs.append(out)
        return jnp.concatenate(outs, axis=1)

    def project_out(j, hidden):
        acc[...] += _dot(hidden, wout_ref[j * ct:(j + 1) * ct, :])

    n_tiles = d_ff // ct
    ys = project(0)
    waiting = []
    for j in range(n_tiles):
        gate = conv(ys[0], j * ct, 2 * (j % 2))
        up = conv(ys[1], d_ff + j * ct, 2 * (j % 2) + 1)
        waiting.append((j, (gate * _sigmoid(gate) * up).astype(BF16)))
        if j + 1 < n_tiles:
            ys = project(j + 1)
        if len(waiting) > OUT_PROJ_LAG:
            project_out(*waiting.pop(0))
    for item in waiting:
        project_out(*item)

    h2 = acc[...]
    gate = _sigmoid(_dot(_rms_norm(h2, pnw_ref[...]).astype(BF16), wgate_ref[...]))
    h3 = h2 + gate * ple
    if final_norm:
        h3 = _rms_norm(h3, onw_ref[...])
    out_ref[...] = h3


def _chan_mix(res, mix, w_mix, ffn_norm, w_in, conv_w, conv_b, w_out, ple_norm, w_gate, p, w_proj,
              out_norm, layer, final_norm):
    bsz, seq, d = res.shape
    ts = SEQ_TILE
    d_ff = w_out.shape[0]
    tok_spec = lambda width: pl.BlockSpec((None, ts, width), lambda b, s: (b, s, 0))
    consts = (w_mix, ffn_norm, w_in, conv_w, conv_b, w_out, ple_norm, w_gate)
    return pl.pallas_call(
        functools.partial(_chan_mix_kernel, ts=ts, d_ff=d_ff, final_norm=final_norm),
        grid=(bsz, seq // ts),
        in_specs=[tok_spec(d), tok_spec(mix.shape[-1])] + [_const_spec(a.shape) for a in consts]
                 + [pl.BlockSpec((None, None, ts, p.shape[-1]), lambda b, s: (layer, b, s, 0)),
                    _const_spec(w_proj.shape), _const_spec(out_norm.shape)],
        out_specs=tok_spec(d),
        out_shape=jax.ShapeDtypeStruct((bsz, seq, d), F32),
        scratch_shapes=[pltpu.VMEM((4, FFN_COL_TILE // LANES, SUBLANES + ts, LANES), F32),
                        pltpu.VMEM((SUBLANES, 2 * d_ff), F32), pltpu.VMEM((ts, d), F32)],
        compiler_params=_params(),
        name="chan_mix",
    )(res, mix, *consts, p, w_proj, out_norm)


def _gelu(x):
    return 0.5 * x * (1.0 + lax.erf(x * (1.0 / math.sqrt(2.0))))


def _sgu_kernel(h_ref, nw_ref, win_ref, lng_ref, lnb_ref, ws_ref, bs_ref, out_ref, vbuf, ubuf,
                *, ts, width):
    hn = _rms_norm(h_ref[...], nw_ref[...]).astype(BF16)
    for buf, base in ((vbuf, width), (ubuf, 0)):
        for c0 in range(0, width, SGU_COL_TILE):
            buf[:, c0:c0 + SGU_COL_TILE] = _gelu(_dot(hn, win_ref[:, base + c0:base + c0 + SGU_COL_TILE]))
    v = vbuf[...]
    mu = jnp.mean(v, axis=-1, keepdims=True)
    vc = v - mu
    inv = lax.rsqrt(jnp.mean(vc * vc, axis=-1, keepdims=True) + LN_EPS)
    vbuf[...] = vc * inv * lng_ref[...] + lnb_ref[...]

    row = lax.broadcasted_iota(jnp.int32, (SGU_CHUNK, SGU_CHUNK), 0)
    col = lax.broadcasted_iota(jnp.int32, (SGU_CHUNK, SGU_CHUNK), 1)
    tri = row >= col
    gd = width // SGU_GROUPS
    for grp in range(SGU_GROUPS):
        cols = slice(grp * gd, (grp + 1) * gd)
        w_tri = jnp.where(tri, ws_ref[grp], 0.0).astype(BF16)
        bias = bs_ref[:, grp:grp + 1]
        for i in range(ts // SGU_CHUNK):
            rows = slice(i * SGU_CHUNK, (i + 1) * SGU_CHUNK)
            mixed = _dot(w_tri, vbuf[rows, cols].astype(BF16)) + bias
            out_ref[rows, cols] = (ubuf[rows, cols] * mixed).astype(BF16)


def _sgu_mix(h, norm_w, w_in, ln_g, ln_b, w_s, b_s_t):
    bsz, seq, d = h.shape
    ts = SEQ_TILE
    width = w_in.shape[1] // 2
    tok_spec = lambda w: pl.BlockSpec((None, ts, w), lambda b, s: (b, s, 0))
    consts = (norm_w, w_in, ln_g, ln_b, w_s, b_s_t)
    return pl.pallas_call(
        functools.partial(_sgu_kernel, ts=ts, width=width),
        grid=(bsz, seq // ts),
        in_specs=[tok_spec(d)] + [_const_spec(a.shape) for a in consts],
        out_specs=tok_spec(width),
        out_shape=jax.ShapeDtypeStruct((bsz, seq, width), BF16),
        scratch_shapes=[pltpu.VMEM((ts, width), F32), pltpu.VMEM((ts, width), F32)],
        compiler_params=_params(),
        name="sgu_mix",
    )(h, *consts)


def kernel(x, p, mix_norm, gdn_w_in, gdn_conv_w, gdn_a_log, gdn_dt_bias, gdn_norm_w, gdn_w_out,
           sgu_w_in, sgu_ln_g, sgu_ln_b, sgu_w_s, sgu_b_s, sgu_w_out,
           ffn_norm, ffn_w_in, ffn_conv_w, ffn_conv_b, ffn_w_out,
           ple_norm, ple_w_gate, ple_w_proj, final_norm):
    row = lambda a: a.reshape(1, -1).astype(F32)
    bf = lambda a: a.astype(BF16)
    hv = GDN_HEADS * GDN_DV
    conv_dim = gdn_conv_w.shape[-1]

    def chan_mix(layer, res, mix, w_mix, final):
        return _chan_mix(res, mix, bf(w_mix), row(ffn_norm[layer]), bf(ffn_w_in[layer]), ffn_conv_w[layer],
                         row(ffn_conv_b[layer]), bf(ffn_w_out[layer]), row(ple_norm[layer]),
                         bf(ple_w_gate[layer]), p, bf(ple_w_proj[layer]), row(final_norm), layer, final)

    w_in = gdn_w_in[0]
    w_main = bf(w_in[:, :conv_dim + hv])
    w_ba_t = bf(w_in[:, conv_dim + hv:].T)
    q, k, v, z, g, beta = _gdn_in(x, row(mix_norm[0]), w_main, w_ba_t, gdn_conv_w[0],
                                  gdn_a_log[0].reshape(-1, 1), gdn_dt_bias[0].reshape(-1, 1))
    o = _gdn_core(q, k, v, z, g, beta, row(gdn_norm_w[0]))
    h = chan_mix(0, x, o, gdn_w_out[0], False)

    um = _sgu_mix(h, row(mix_norm[1]), bf(sgu_w_in[0]), row(sgu_ln_g[0]), row(sgu_ln_b[0]),
                  sgu_w_s[0], sgu_b_s[0].T)
    return chan_mix(1, h, um, sgu_w_out[0], True)
```
